```python
import math
import jax, jax.numpy as jnp
from jax import lax
import numpy as np

D_MODEL = 1024
BATCH = 2
SEQ = 16384
DEPTH = 1
DEC_BATCH = 16
DEC_SEQ = 4096
PAST_LEN = 128

PLE_DIM = 256
D_FF = 2816
EPS = 1e-6
MLA_HEADS = 8
QK_NOPE = 64
QK_ROPE = 32
V_HEAD = 64
Q_LORA = 384
KV_LORA = 256
MLA_WIDTH = MLA_HEADS * V_HEAD
ROPE_BASE = 10000.0
Q_BLOCK = 128
HG_HEADS = 4
HG_DK = 128
HG_DV = 128
HG_KDIM = HG_HEADS * HG_DK
HG_WIDTH = HG_HEADS * HG_DV
CHUNK = 64
D_MIX = MLA_WIDTH + HG_WIDTH
IN_SIZES = (Q_LORA, KV_LORA, QK_ROPE, HG_KDIM, HG_WIDTH, HG_KDIM, HG_KDIM, HG_WIDTH)
D_IN = Q_LORA + KV_LORA + QK_ROPE + 3 * HG_KDIM + 2 * HG_WIDTH

kernel_name = "hybrid_mla_hgrn2_macaron_encoder"


def _rms(x, g):
    xf = x.astype(jnp.float32)
    y = xf * lax.rsqrt(jnp.mean(xf * xf, axis=-1, keepdims=True) + EPS)
    return y.astype(x.dtype) * g


def _swiglu(x, wg, wu, wd):
    return (jax.nn.silu(x @ wg) * (x @ wu)) @ wd


def _rope(x, cos, sin):
    x1, x2 = jnp.split(x, 2, axis=-1)
    return jnp.concatenate([x1 * cos - x2 * sin, x1 * sin + x2 * cos], axis=-1)


def _split_cols(u):
    out = []
    start = 0
    for n in IN_SIZES:
        out.append(u[..., start:start + n])
        start += n
    return out


def _mla(c_q, c_kv, k_r, q_norm, kv_norm, w_uq, w_uk, w_uv):
    B, S, _ = c_q.shape
    q = (_rms(c_q, q_norm) @ w_uq).reshape(B, S, MLA_HEADS, QK_NOPE + QK_ROPE)
    q_nope, q_rope = q[..., :QK_NOPE], q[..., QK_NOPE:]
    ckv = _rms(c_kv, kv_norm)
    k_nope = (ckv @ w_uk).reshape(B, S, MLA_HEADS, QK_NOPE)
    v = (ckv @ w_uv).reshape(B, S, MLA_HEADS, V_HEAD)
    inv_freq = jnp.exp(jnp.arange(0, QK_ROPE, 2, dtype=jnp.float32) * (-math.log(ROPE_BASE) / QK_ROPE))
    ang = jnp.arange(S, dtype=jnp.float32)[:, None] * inv_freq[None, :]
    cos = jnp.cos(ang).astype(c_q.dtype)
    sin = jnp.sin(ang).astype(c_q.dtype)
    q_rope = _rope(q_rope, cos[:, None, :], sin[:, None, :])
    k_r = _rope(k_r, cos, sin)
    nb = S // Q_BLOCK
    qn = q_nope.reshape(B, nb, Q_BLOCK, MLA_HEADS, QK_NOPE).transpose(1, 0, 2, 3, 4)
    qr = q_rope.reshape(B, nb, Q_BLOCK, MLA_HEADS, QK_ROPE).transpose(1, 0, 2, 3, 4)
    scale = (QK_NOPE + QK_ROPE) ** -0.5

    def block(args):
        qn_b, qr_b = args
        s = (jnp.einsum('bqhd,bkhd->bhqk', qn_b, k_nope)
             + jnp.einsum('bqhr,bkr->bhqk', qr_b, k_r))
        pr = jax.nn.softmax(s.astype(jnp.float32) * scale, axis=-1).astype(v.dtype)
        return jnp.einsum('bhqk,bkhd->bqhd', pr, v)

    o = lax.map(block, (qn, qr))
    return o.transpose(1, 0, 2, 3, 4).reshape(B, S, MLA_WIDTH)


def _hgrn_chunk_scan(q, k, v, logf):
    B, H, S, dk = q.shape
    dv = v.shape[-1]
    n = S // CHUNK

    def chunks(t):
        return t.reshape(B, H, n, CHUNK, t.shape[-1]).transpose(2, 0, 1, 3, 4)

    lower = jnp.tril(jnp.ones((CHUNK, CHUNK), dtype=bool))[:, :, None]

    def step(state, inp):
        qc, kc, vc, gc = inp
        b = jnp.cumsum(gc, axis=2)
        o_inter = jnp.einsum('bhtk,bhkv->bhtv', qc * jnp.exp(b), state)
        diff = b[:, :, :, None, :] - b[:, :, None, :, :]
        dec = jnp.exp(jnp.where(lower, diff, -jnp.inf))
        a = jnp.einsum('bhtsk,bhsk->bhts', qc[:, :, :, None, :] * dec, kc)
        o = o_inter + jnp.einsum('bhts,bhsv->bhtv', a, vc)
        b_last = b[:, :, -1:, :]
        state = (jnp.exp(b_last[:, :, 0, :])[..., None] * state
                 + jnp.einsum('bhsk,bhsv->bhkv', kc * jnp.exp(b_last - b), vc))
        return state, o

    state0 = jnp.zeros((B, H, dk, dv), jnp.float32)
    _, o = lax.scan(step, state0, (chunks(q), chunks(k), chunks(v), chunks(logf)))
    return o.transpose(1, 2, 0, 3, 4).reshape(B, H, S, dv)


def _hgrn(h_q, h_i, h_ff, h_fb, h_g, lb_f, lb_b, o_norm):
    B, S, _ = h_q.shape

    def heads(t):
        return t.astype(jnp.float32).reshape(B, S, HG_HEADS, -1).transpose(0, 2, 1, 3)

    q = heads(jax.nn.silu(h_q))
    v = heads(h_i)

    def decay(h_f, lb):
        z = h_f.astype(jnp.float32)
        lb = lb.astype(jnp.float32)
        f = lb + (1.0 - lb) * jax.nn.sigmoid(z)
        return heads((1.0 - lb) * jax.nn.sigmoid(-z)), heads(jnp.log(f))

    k_f, g_f = decay(h_ff, lb_f)
    k_b, g_b = decay(h_fb, lb_b)
    o_fwd = _hgrn_chunk_scan(q, k_f, v, g_f)
    o_bwd = jnp.flip(_hgrn_chunk_scan(jnp.flip(q, axis=2), jnp.flip(k_b, axis=2),
                                      jnp.flip(v, axis=2), jnp.flip(g_b, axis=2)), axis=2)
    o = o_fwd + o_bwd
    o = (o * lax.rsqrt(jnp.mean(o * o, axis=-1, keepdims=True) + EPS)
         * o_norm.astype(jnp.float32).reshape(HG_HEADS, 1, HG_DV))
    o = o.transpose(0, 2, 1, 3).reshape(B, S, HG_WIDTH).astype(h_q.dtype)
    return o * jax.nn.silu(h_g)


def _trunk(x, p, ffn1_norm, ffn1_wg, ffn1_wu, ffn1_wd, mix_norm, w_in, q_norm, w_uq,
           kv_norm, w_uk, w_uv, hg_lb, hg_norm, w_o, ffn2_norm, ffn2_wg, ffn2_wu, ffn2_wd,
           ple_norm, w_ple_gate, w_ple_proj, final_norm):
    lb = jnp.cumsum(jax.nn.softmax(hg_lb.astype(jnp.float32), axis=1), axis=1)
    h = x
    for l in range(DEPTH):
        h = h + 0.5 * _swiglu(_rms(h, ffn1_norm[l]), ffn1_wg[l], ffn1_wu[l], ffn1_wd[l])
        u = _rms(h, mix_norm[l]) @ w_in[l]
        c_q, c_kv, k_r, h_q, h_i, h_ff, h_fb, h_g = _split_cols(u)
        mix = jnp.concatenate([
            _mla(c_q, c_kv, k_r, q_norm[l], kv_norm[l], w_uq[l], w_uk[l], w_uv[l]),
            _hgrn(h_q, h_i, h_ff, h_fb, h_g, lb[0, l], lb[1, l], hg_norm[l]),
        ], axis=-1)
        h = h + mix @ w_o[l]
        h = h + 0.5 * _swiglu(_rms(h, ffn2_norm[l]), ffn2_wg[l], ffn2_wu[l], ffn2_wd[l])
        gate = jax.nn.sigmoid(_rms(h, ple_norm[l]) @ w_ple_gate[l])
        h = h + gate * (p[l].astype(h.dtype) @ w_ple_proj[l])
    return _rms(h, final_norm)


def setup_inputs(seed: int = 0) -> dict:
    key = jax.random.key(seed)
    ks = jax.random.split(key, 32)
    f32 = jnp.float32

    def nrm(k, shape, fan_in):
        return jax.random.normal(k, shape, f32) * (fan_in ** -0.5)

    def gain(k, shape):
        return 1.0 + 0.05 * jax.random.normal(k, shape, f32)

    L = DEPTH
    return {
        "x_prompt": jax.random.normal(ks[0], (BATCH, SEQ, D_MODEL), f32),
        "x_sample": jax.random.normal(ks[1], (DEC_BATCH, DEC_SEQ, D_MODEL), f32),
        "p_prompt": jax.random.normal(ks[2], (DEPTH, BATCH, SEQ, PLE_DIM), f32),
        "p_sample": jax.random.normal(ks[3], (DEPTH, DEC_BATCH, DEC_SEQ, PLE_DIM), f32),
        "ffn1_norm": gain(ks[4], (L, D_MODEL)),
        "ffn1_wg": nrm(ks[5], (L, D_MODEL, D_FF), D_MODEL),
        "ffn1_wu": nrm(ks[6], (L, D_MODEL, D_FF), D_MODEL),
        "ffn1_wd": nrm(ks[7], (L, D_FF, D_MODEL), D_FF),
        "mix_norm": gain(ks[8], (L, D_MODEL)),
        "w_in": nrm(ks[9], (L, D_MODEL, D_IN), D_MODEL),
        "q_norm": gain(ks[10], (L, Q_LORA)),
        "w_uq": nrm(ks[11], (L, Q_LORA, MLA_HEADS * (QK_NOPE + QK_ROPE)), Q_LORA),
        "kv_norm": gain(ks[12], (L, KV_LORA)),
        "w_uk": nrm(ks[13], (L, KV_LORA, MLA_HEADS * QK_NOPE), KV_LORA),
        "w_uv": nrm(ks[14], (L, KV_LORA, MLA_HEADS * V_HEAD), KV_LORA),
        "hg_lb": 0.5 * jax.random.normal(ks[15], (2, L + 1, HG_KDIM), f32),
        "hg_norm": gain(ks[16], (L, HG_WIDTH)),
        "w_o": nrm(ks[17], (L, D_MIX, D_MODEL), D_MIX),
        "ffn2_norm": gain(ks[18], (L, D_MODEL)),
        "ffn2_wg": nrm(ks[19], (L, D_MODEL, D_FF), D_MODEL),
        "ffn2_wu": nrm(ks[20], (L, D_MODEL, D_FF), D_MODEL),
        "ffn2_wd": nrm(ks[21], (L, D_FF, D_MODEL), D_FF),
        "ple_norm": gain(ks[22], (L, D_MODEL)),
        "w_ple_gate": nrm(ks[23], (L, D_MODEL, D_MODEL), D_MODEL),
        "w_ple_proj": nrm(ks[24], (L, PLE_DIM, D_MODEL), PLE_DIM),
        "final_norm": gain(ks[25], (D_MODEL,)),
    }


def reference(x_prompt, x_sample, p_prompt, p_sample, ffn1_norm, ffn1_wg, ffn1_wu, ffn1_wd,
              mix_norm, w_in, q_norm, w_uq, kv_norm, w_uk, w_uv, hg_lb, hg_norm, w_o,
              ffn2_norm, ffn2_wg, ffn2_wu, ffn2_wd, ple_norm, w_ple_gate, w_ple_proj, final_norm):
    w = (ffn1_norm, ffn1_wg, ffn1_wu, ffn1_wd, mix_norm, w_in, q_norm, w_uq, kv_norm, w_uk, w_uv,
         hg_lb, hg_norm, w_o, ffn2_norm, ffn2_wg, ffn2_wu, ffn2_wd, ple_norm, w_ple_gate,
         w_ple_proj, final_norm)
    y_prompt = _trunk(x_prompt, p_prompt, *w)
    y_sample = _trunk(x_sample, p_sample, *w)
    return (y_prompt, y_sample)
```

```python
import functools
import math

import numpy as np
import jax
import jax.numpy as jnp
from jax import lax
from jax.experimental import pallas as pl
from jax.experimental.pallas import tpu as pltpu

F32 = jnp.float32
BF16 = jnp.bfloat16

EPS = 1e-6
MLA_HEADS = 8
QK_NOPE = 64
QK_ROPE = 32
V_HEAD = 64
Q_LORA = 384
KV_LORA = 256
ROPE_BASE = 10000.0
HG_HEADS = 4
HG_DK = 128
HG_DV = 128
HG_WIDTH = HG_HEADS * HG_DV
MLA_WIDTH = MLA_HEADS * V_HEAD

LANES = 128
HEAD_PAD = LANES
FF_TILE = 256
VMEM_LIMIT = 58 * 1024 * 1024

NT_DIMS = (((1,), (1,)), ((), ()))
TN_DIMS = (((0,), (0,)), ((), ()))


def _rms(x, g):
    return x * lax.rsqrt(jnp.mean(x * x, axis=-1, keepdims=True) + EPS) * g


def _sigmoid(x):
    return 1.0 / (1.0 + jnp.exp(-x))


def _swiglu(xn, wg_ref, wu_ref, wd_ref):
    d_ff = wg_ref.shape[1]
    acc = None
    for f0 in range(0, d_ff, FF_TILE):
        g = jnp.dot(xn, wg_ref[:, f0:f0 + FF_TILE], preferred_element_type=F32)
        u = jnp.dot(xn, wu_ref[:, f0:f0 + FF_TILE], preferred_element_type=F32)
        a = (g * _sigmoid(g) * u).astype(BF16)
        part = jnp.dot(a, wd_ref[f0:f0 + FF_TILE, :], preferred_element_type=F32)
        acc = part if acc is None else acc + part
    return acc


def _pre_kernel(x_ref, ct_ref, st_ref, c_ref, s_ref, n1_ref, wg_ref, wu_ref, wd_ref, nmix_ref,
                win_ref, qn_ref, wqat_ref, wqbt_ref, kvn_ref, wuk_ref, wuvt_ref,
                h_ref, qt_ref, k_ref, vt_ref, uh_ref):
    x = x_ref[0]
    h = x + 0.5 * _swiglu(_rms(x, n1_ref[...]).astype(BF16), wg_ref, wu_ref, wd_ref)
    h_ref[0] = h
    hn = _rms(h, nmix_ref[...]).astype(BF16)
    u = jnp.dot(hn, win_ref[...], preferred_element_type=F32)
    o_kv = Q_LORA
    o_ka = o_kv + KV_LORA
    o_kb = o_ka + HEAD_PAD
    o_h = o_kb + HEAD_PAD
    uh_ref[0] = u[:, o_h:]

    cqn = _rms(u[:, :o_kv], qn_ref[...]).astype(BF16)
    qat = lax.dot_general(wqat_ref[...], cqn, NT_DIMS, preferred_element_type=F32)
    qbt = lax.dot_general(wqbt_ref[...], cqn, NT_DIMS, preferred_element_type=F32)
    ct = ct_ref[...]
    st = st_ref[...]
    scale = (QK_NOPE + QK_ROPE) ** -0.5
    for hd in range(MLA_HEADS):
        r0 = hd * HEAD_PAD
        qh = (qat[r0:r0 + HEAD_PAD] * ct + qbt[r0:r0 + HEAD_PAD] * st) * scale
        qt_ref[0, r0:r0 + HEAD_PAD, :] = qh.astype(BF16)

    ckvn = _rms(u[:, o_kv:o_ka], kvn_ref[...]).astype(BF16)
    kn = jnp.dot(ckvn, wuk_ref[...], preferred_element_type=F32)
    kr = u[:, o_ka:o_kb] * c_ref[...] + u[:, o_kb:o_h] * s_ref[...]
    for hd in range(MLA_HEADS):
        r0 = hd * HEAD_PAD
        k_ref[0, :, r0:r0 + HEAD_PAD] = (kn[:, r0:r0 + HEAD_PAD] + kr).astype(BF16)

    vt = lax.dot_general(wuvt_ref[...], ckvn, NT_DIMS, preferred_element_type=F32)
    for hd in range(MLA_HEADS):
        vt_ref[0, hd, 0] = vt[hd * V_HEAD:(hd + 1) * V_HEAD].astype(BF16)


def _pre_call(x, tabs, wts, tm):
    B, S, D = x.shape
    nt = S // tm
    ct, st, c, s = tabs
    n_h = wts["w_in"].shape[1] - (Q_LORA + KV_LORA + 2 * HEAD_PAD)
    vm = pl.BlockSpec(memory_space=pltpu.VMEM)
    in_specs = [
        pl.BlockSpec((1, tm, D), lambda b, i: (b, i, 0)),
        pl.BlockSpec((HEAD_PAD, tm), lambda b, i: (0, i)),
        pl.BlockSpec((HEAD_PAD, tm), lambda b, i: (0, i)),
        pl.BlockSpec((tm, HEAD_PAD), lambda b, i: (i, 0)),
        pl.BlockSpec((tm, HEAD_PAD), lambda b, i: (i, 0)),
    ] + [vm] * 12
    out_shape = [
        jax.ShapeDtypeStruct((B, S, D), F32),
        jax.ShapeDtypeStruct((B, MLA_HEADS * HEAD_PAD, S), BF16),
        jax.ShapeDtypeStruct((B, S, MLA_HEADS * HEAD_PAD), BF16),
        jax.ShapeDtypeStruct((B, MLA_HEADS, nt, V_HEAD, tm), BF16),
        jax.ShapeDtypeStruct((B, S, n_h), F32),
    ]
    out_specs = [
        pl.BlockSpec((1, tm, D), lambda b, i: (b, i, 0)),
        pl.BlockSpec((1, MLA_HEADS * HEAD_PAD, tm), lambda b, i: (b, 0, i)),
        pl.BlockSpec((1, tm, MLA_HEADS * HEAD_PAD), lambda b, i: (b, i, 0)),
        pl.BlockSpec((1, MLA_HEADS, 1, V_HEAD, tm), lambda b, i: (b, 0, i, 0, 0)),
        pl.BlockSpec((1, tm, n_h), lambda b, i: (b, i, 0)),
    ]
    return pl.pallas_call(
        _pre_kernel,
        grid=(B, nt),
        in_specs=in_specs,
        out_specs=out_specs,
        out_shape=out_shape,
        compiler_params=pltpu.CompilerParams(
            dimension_semantics=("arbitrary", "arbitrary"), vmem_limit_bytes=VMEM_LIMIT),
        name="pre",
    )(x, ct, st, c, s, wts["ffn1_norm"], wts["ffn1_wg"], wts["ffn1_wu"], wts["ffn1_wd"],
      wts["mix_norm"], wts["w_in"], wts["q_norm"], wts["w_qat"], wts["w_qbt"], wts["kv_norm"],
      wts["w_ukp"], wts["w_uvt"])


def _attn_kernel(qt_ref, k_ref, vt_ref, o_ref, *, tk, vt_w):
    q = qt_ref[0]
    tq = q.shape[1]
    n_kv = k_ref.shape[1] // tk
    n_sub = tk // vt_w

    def body(i, carry):
        m, l, acc = carry
        k = k_ref[0, pl.ds(pl.multiple_of(i * tk, tk), tk), :]
        s = jnp.dot(k, q, preferred_element_type=F32)
        m_new = jnp.maximum(m, jnp.max(s, axis=0, keepdims=True))
        alpha = jnp.exp(m - m_new)
        p = jnp.exp(s - m_new)
        l = alpha * l + jnp.sum(p, axis=0, keepdims=True)
        pb = p.astype(BF16)
        acc = alpha * acc
        for j in range(n_sub):
            acc = acc + jnp.dot(vt_ref[0, 0, i * n_sub + j], pb[j * vt_w:(j + 1) * vt_w],
                                preferred_element_type=F32)
        return m_new, l, acc

    m0 = jnp.full((1, tq), -1e30, F32)
    l0 = jnp.zeros((1, tq), F32)
    a0 = jnp.zeros((V_HEAD, tq), F32)
    _, l, acc = lax.fori_loop(0, n_kv, body, (m0, l0, a0))
    o_ref[0] = acc * (1.0 / l)


def _attn_call(qt, k, vt, tq, tk):
    B, _, S = qt.shape
    vt_w = vt.shape[-1]
    n_vt = vt.shape[2]
    return pl.pallas_call(
        functools.partial(_attn_kernel, tk=tk, vt_w=vt_w),
        grid=(B, MLA_HEADS, S // tq),
        in_specs=[
            pl.BlockSpec((1, HEAD_PAD, tq), lambda b, h, i: (b, h, i)),
            pl.BlockSpec((1, S, HEAD_PAD), lambda b, h, i: (b, 0, h)),
            pl.BlockSpec((1, 1, n_vt, V_HEAD, vt_w), lambda b, h, i: (b, h, 0, 0, 0)),
        ],
        out_specs=pl.BlockSpec((1, V_HEAD, tq), lambda b, h, i: (b, h, i)),
        out_shape=jax.ShapeDtypeStruct((B, MLA_WIDTH, S), F32),
        compiler_params=pltpu.CompilerParams(
            dimension_semantics=("arbitrary", "arbitrary", "arbitrary"),
            vmem_limit_bytes=VMEM_LIMIT),
        name="attn",
    )(qt, k, vt)


def _hgrn_levels(chunk):
    return [chunk >> (j + 1) for j in range(int(math.log2(chunk)))]


def _hgrn_consts(chunk):
    t = np.arange(chunk)
    mats = [(t[None, :] <= t[:, None]).astype(np.float32)]
    for m in _hgrn_levels(chunk):
        ref = (t // (2 * m)) * 2 * m + m - 1
        later = (t % (2 * m)) >= m
        u = t[None, :]
        d = np.where(later[:, None], (u > ref[:, None]) & (u <= t[:, None]),
                     (u > t[:, None]) & (u <= ref[:, None]))
        mats.append(d.astype(np.float32))
    mats.append(np.ones((8, chunk), np.float32))
    fwd = np.concatenate(mats, axis=0)
    n_blk = len(mats) - 1
    bwd = np.concatenate([np.flip(mm, (0, 1)) for mm in mats[:n_blk]] + [mats[-1]], axis=0)
    return np.stack([fwd, bwd])


def _hgrn_kernel(q_ref, v_ref, z_ref, lb_ref, d_ref, o_ref, st_ref, *, chunk):
    direction = pl.program_id(0)

    @pl.when(pl.program_id(3) == 0)
    def _():
        st_ref[...] = jnp.zeros_like(st_ref)

    levels = _hgrn_levels(chunk)
    lb = lb_ref[0]
    qr = q_ref[0]
    v = v_ref[0]
    z = z_ref[0]
    q = qr * _sigmoid(qr)
    k = (1.0 - lb) * _sigmoid(-z)
    g = jnp.log(lb + (1.0 - lb) * _sigmoid(z))

    g_hi = g.astype(BF16)
    r1 = g - g_hi.astype(F32)
    g_mid = r1.astype(BF16)
    g_lo = (r1 - g_mid.astype(F32)).astype(BF16)
    g3 = jnp.concatenate([g_hi, g_mid, g_lo], axis=1)
    e3 = jnp.dot(d_ref[0], g3, preferred_element_type=F32)
    e = e3[:, :HG_DK] + (e3[:, HG_DK:2 * HG_DK] + e3[:, 2 * HG_DK:])
    b = e[:chunk]
    tot = e[(len(levels) + 1) * chunk:(len(levels) + 1) * chunk + 1]

    st = st_ref[...]
    vb = v.astype(BF16)
    o = lax.dot_general((q * jnp.exp(b)).astype(BF16), st.astype(BF16), NT_DIMS,
                        preferred_element_type=F32)

    row = lax.broadcasted_iota(jnp.int32, (chunk, chunk), 0)
    col = lax.broadcasted_iota(jnp.int32, (chunk, chunk), 1)
    x = row ^ col
    earlier = (row - col) * (1 - 2 * direction) > 0
    a = jnp.where(x == 0, jnp.sum(q * k, axis=-1, keepdims=True), 0.0)
    for li, m in enumerate(levels):
        w = jnp.exp(e[(li + 1) * chunk:(li + 2) * chunk])
        al = lax.dot_general((q * w).astype(BF16), (k * w).astype(BF16), NT_DIMS,
                             preferred_element_type=F32)
        a = a + jnp.where(earlier & (x >= m) & (x < 2 * m), al, 0.0)
    o = o + jnp.dot(a.astype(BF16), vb, preferred_element_type=F32)
    o_ref[0, 0] = o

    kd = (k * jnp.exp(jnp.minimum(tot - b, 0.0))).astype(BF16)
    st_ref[...] = st * jnp.exp(tot) + lax.dot_general(vb, kd, TN_DIMS, preferred_element_type=F32)


def _hgrn_call(uh, lb3, dmat, chunk):
    B, S, _ = uh.shape
    n = S // chunk
    n_lb = lb3.shape[0] // 2

    def t_idx(d, i):
        return i + d * (n - 1 - 2 * i)

    return pl.pallas_call(
        functools.partial(_hgrn_kernel, chunk=chunk),
        grid=(2, B, HG_HEADS, n),
        in_specs=[
            pl.BlockSpec((1, chunk, HG_DK), lambda d, b, h, i: (b, t_idx(d, i), h)),
            pl.BlockSpec((1, chunk, HG_DV), lambda d, b, h, i: (b, t_idx(d, i), HG_HEADS + h)),
            pl.BlockSpec((1, chunk, HG_DK), lambda d, b, h, i: (b, t_idx(d, i), (2 + d) * HG_HEADS + h)),
            pl.BlockSpec((1, 1, HG_DK), lambda d, b, h, i: (d * n_lb, 0, h)),
            pl.BlockSpec((1,) + dmat.shape[1:], lambda d, b, h, i: (d, 0, 0)),
        ],
        out_specs=pl.BlockSpec((1, 1, chunk, HG_DV), lambda d, b, h, i: (d, b, t_idx(d, i), h)),
        out_shape=jax.ShapeDtypeStruct((2, B, S, HG_WIDTH), F32),
        scratch_shapes=[pltpu.VMEM((HG_DV, HG_DK), F32)],
        compiler_params=pltpu.CompilerParams(
            dimension_semantics=("arbitrary", "arbitrary", "arbitrary", "arbitrary"),
            vmem_limit_bytes=VMEM_LIMIT),
        name="hgrn",
    )(uh, uh, uh, lb3, dmat)


def _post_kernel(h1_ref, ot_ref, of_ref, ob_ref, hg_ref, p_ref, hgn_ref, wot_ref, wob_ref,
                 n2_ref, wg_ref, wu_ref, wd_ref, npl_ref, wpg_ref, wpp_ref, nf_ref, y_ref):
    o = of_ref[0, 0] + ob_ref[0, 0]
    gt = hg_ref[0]
    parts = []
    for hd in range(HG_HEADS):
        oh = o[:, hd * HG_DV:(hd + 1) * HG_DV]
        parts.append(_rms(oh, hgn_ref[:, hd * HG_DV:(hd + 1) * HG_DV]))
    hmix = (jnp.concatenate(parts, axis=1) * (gt * _sigmoid(gt))).astype(BF16)
    h = (h1_ref[0]
         + lax.dot_general(ot_ref[0].astype(BF16), wot_ref[...], TN_DIMS, preferred_element_type=F32)
         + jnp.dot(hmix, wob_ref[...], preferred_element_type=F32))
    h = h + 0.5 * _swiglu(_rms(h, n2_ref[...]).astype(BF16), wg_ref, wu_ref, wd_ref)
    gate = _sigmoid(jnp.dot(_rms(h, npl_ref[...]).astype(BF16), wpg_ref[...],
                            preferred_element_type=F32))
    h = h + gate * jnp.dot(p_ref[0, 0].astype(BF16), wpp_ref[...], preferred_element_type=F32)
    y_ref[0] = _rms(h, nf_ref[...])


def _post_call(h1, ot, o2, uh, p, layer, wts, tm):
    B, S, D = h1.shape
    g_blk = (uh.shape[2] - HG_WIDTH) // HG_WIDTH
    vm = pl.BlockSpec(memory_space=pltpu.VMEM)
    in_specs = [
        pl.BlockSpec((1, tm, D), lambda b, i: (b, i, 0)),
        pl.BlockSpec((1, MLA_WIDTH, tm), lambda b, i: (b, 0, i)),
        pl.BlockSpec((1, 1, tm, HG_WIDTH), lambda b, i: (0, b, i, 0)),
        pl.BlockSpec((1, 1, tm, HG_WIDTH), lambda b, i: (1, b, i, 0)),
        pl.BlockSpec((1, tm, HG_WIDTH), lambda b, i: (b, i, g_blk)),
        pl.BlockSpec((1, 1, tm, p.shape[-1]), lambda b, i: (layer, b, i, 0)),
    ] + [vm] * 11
    return pl.pallas_call(
        _post_kernel,
        grid=(B, S // tm),
        in_specs=in_specs,
        out_specs=pl.BlockSpec((1, tm, D), lambda b, i: (b, i, 0)),
        out_shape=jax.ShapeDtypeStruct((B, S, D), F32),
        compiler_params=pltpu.CompilerParams(
            dimension_semantics=("arbitrary", "arbitrary"), vmem_limit_bytes=VMEM_LIMIT),
        name="post",
    )(h1, ot, o2, o2, uh, p, wts["hg_norm"], wts["w_o_top"], wts["w_o_bot"], wts["ffn2_norm"],
      wts["ffn2_wg"], wts["ffn2_wu"], wts["ffn2_wd"], wts["ple_norm"], wts["w_ple_gate"],
      wts["w_ple_proj"], wts["final_norm"])


def _rope_tables(S):
    half = QK_ROPE // 2
    inv_freq = jnp.exp(jnp.arange(0, QK_ROPE, 2, dtype=F32) * (-math.log(ROPE_BASE) / QK_ROPE))
    ang = jnp.arange(S, dtype=F32)[:, None] * inv_freq[None, :]
    cos, sin = jnp.cos(ang), jnp.sin(ang)
    ones = jnp.ones((S, QK_NOPE), F32)
    zpad = jnp.zeros((S, HEAD_PAD - QK_NOPE - QK_ROPE), F32)
    c = jnp.concatenate([ones, cos, cos, zpad], axis=1)
    s = jnp.concatenate([0.0 * ones, -sin, sin, zpad], axis=1)
    assert half * 2 == QK_ROPE
    return c.T, s.T, c, s


def _layer_weights(l, ffn1_norm, ffn1_wg, ffn1_wu, ffn1_wd, mix_norm, w_in, q_norm, w_uq, kv_norm,
                   w_uk, w_uv, hg_norm, w_o, ffn2_norm, ffn2_wg, ffn2_wu, ffn2_wd, ple_norm,
                   w_ple_gate, w_ple_proj, final_norm):
    D = w_in.shape[1]
    half = QK_ROPE // 2
    o_kr = Q_LORA + KV_LORA
    wi = w_in[l]
    z64 = jnp.zeros((D, QK_NOPE), F32)
    z32 = jnp.zeros((D, HEAD_PAD - QK_NOPE - QK_ROPE), F32)
    x1, x2 = wi[:, o_kr:o_kr + half], wi[:, o_kr + half:o_kr + QK_ROPE]
    w_in_p = jnp.concatenate([wi[:, :o_kr], z64, x1, x2, z32, z64, x2, x1, z32,
                              wi[:, o_kr + QK_ROPE:]], axis=1)

    wq = w_uq[l].reshape(Q_LORA, MLA_HEADS, QK_NOPE + QK_ROPE)
    qn, q1, q2 = wq[..., :QK_NOPE], wq[..., QK_NOPE:QK_NOPE + half], wq[..., QK_NOPE + half:]
    zq = jnp.zeros((Q_LORA, MLA_HEADS, HEAD_PAD - QK_NOPE - QK_ROPE), F32)
    wqa = jnp.concatenate([qn, q1, q2, zq], axis=-1).reshape(Q_LORA, MLA_HEADS * HEAD_PAD)
    wqb = jnp.concatenate([0.0 * qn, q2, q1, zq], axis=-1).reshape(Q_LORA, MLA_HEADS * HEAD_PAD)

    wk = w_uk[l].reshape(KV_LORA, MLA_HEADS, QK_NOPE)
    wkp = jnp.concatenate([wk, jnp.zeros((KV_LORA, MLA_HEADS, HEAD_PAD - QK_NOPE), F32)],
                          axis=-1).reshape(KV_LORA, MLA_HEADS * HEAD_PAD)

    row = lambda a: a.reshape(1, -1)
    return {
        "ffn1_norm": row(ffn1_norm[l]), "ffn1_wg": ffn1_wg[l].astype(BF16),
        "ffn1_wu": ffn1_wu[l].astype(BF16), "ffn1_wd": ffn1_wd[l].astype(BF16),
        "mix_norm": row(mix_norm[l]), "w_in": w_in_p.astype(BF16),
        "q_norm": row(q_norm[l]), "w_qat": wqa.T.astype(BF16), "w_qbt": wqb.T.astype(BF16),
        "kv_norm": row(kv_norm[l]), "w_ukp": wkp.astype(BF16), "w_uvt": w_uv[l].T.astype(BF16),
        "hg_norm": row(hg_norm[l]),
        "w_o_top": w_o[l][:MLA_WIDTH].astype(BF16), "w_o_bot": w_o[l][MLA_WIDTH:].astype(BF16),
        "ffn2_norm": row(ffn2_norm[l]), "ffn2_wg": ffn2_wg[l].astype(BF16),
        "ffn2_wu": ffn2_wu[l].astype(BF16), "ffn2_wd": ffn2_wd[l].astype(BF16),
        "ple_norm": row(ple_norm[l]), "w_ple_gate": w_ple_gate[l].astype(BF16),
        "w_ple_proj": w_ple_proj[l].astype(BF16), "final_norm": row(final_norm),
    }


def _tiles(S):
    tm = min(256, S)
    tq = min(256, S)
    tk = min(512, S)
    chunk = min(64, S)
    return tm, tq, tk, chunk


def _trunk(x, p, layers, lb3, depth):
    B, S, D = x.shape
    tm, tq, tk, chunk = _tiles(S)
    tabs = _rope_tables(S)
    dmat = jnp.asarray(_hgrn_consts(chunk), BF16)
    h = x
    for l in range(depth):
        wts = layers[l]
        h1, qt, k, vt, uh = _pre_call(h, tabs, wts, tm)
        ot = _attn_call(qt, k, vt, tq, tk)
        o2 = _hgrn_call(uh, lb3[l], dmat, chunk)
        h = _post_call(h1, ot, o2, uh, p, l, wts, tm)
    return h


def kernel(x_prompt, x_sample, p_prompt, p_sample, ffn1_norm, ffn1_wg, ffn1_wu, ffn1_wd, mix_norm,
           w_in, q_norm, w_uq, kv_norm, w_uk, w_uv, hg_lb, hg_norm, w_o, ffn2_norm, ffn2_wg,
           ffn2_wu, ffn2_wd, ple_norm, w_ple_gate, w_ple_proj, final_norm):
    depth = w_in.shape[0]
    assert depth == 1, "the final norm is fused into the last layer's post kernel"
    layers = [
        _layer_weights(l, ffn1_norm, ffn1_wg, ffn1_wu, ffn1_wd, mix_norm, w_in, q_norm, w_uq,
                       kv_norm, w_uk, w_uv, hg_norm, w_o, ffn2_norm, ffn2_wg, ffn2_wu, ffn2_wd,
                       ple_norm, w_ple_gate, w_ple_proj, final_norm)
        for l in range(depth)
    ]
    lb = jnp.cumsum(jax.nn.softmax(hg_lb.astype(F32), axis=1), axis=1)
    lb3 = [jnp.stack([lb[0, l], lb[1, l]]).reshape(2, 1, HG_WIDTH) for l in range(depth)]
    y_prompt = _trunk(x_prompt, p_prompt, layers, lb3, depth)
    y_sample = _trunk(x_sample, p_sample, layers, lb3, depth)
    return (y_prompt, y_sample)
```

```python
import functools
import math

import numpy as np
import jax
import jax.numpy as jnp
from jax import lax
from jax.experimental import pallas as pl
from jax.experimental.pallas import tpu as pltpu

F32 = jnp.float32
BF16 = jnp.bfloat16

EPS = 1e-6
MLA_HEADS = 8
QK_NOPE = 64
QK_ROPE = 32
V_HEAD = 64
Q_LORA = 384
KV_LORA = 256
ROPE_BASE = 10000.0
HG_HEADS = 4
HG_DK = 128
HG_DV = 128
HG_WIDTH = HG_HEADS * HG_DV
MLA_WIDTH = MLA_HEADS * V_HEAD

LANES = 128
HEAD_PAD = LANES
FF_TILE = 256
VMEM_LIMIT = 58 * 1024 * 1024
LOG2_E = 1.4426950408889634
KV_UNROLL = 4

NT_DIMS = (((1,), (1,)), ((), ()))
TN_DIMS = (((0,), (0,)), ((), ()))


def _rms(x, g):
    return x * lax.rsqrt(jnp.mean(x * x, axis=-1, keepdims=True) + EPS) * g


def _sigmoid(x):
    return 1.0 / (1.0 + jnp.exp(-x))


def _swiglu(xn, wg_ref, wu_ref, wd_ref):
    d_ff = wg_ref.shape[1]
    acc = None
    for f0 in range(0, d_ff, FF_TILE):
        g = jnp.dot(xn, wg_ref[:, f0:f0 + FF_TILE], preferred_element_type=F32)
        u = jnp.dot(xn, wu_ref[:, f0:f0 + FF_TILE], preferred_element_type=F32)
        a = (g * _sigmoid(g) * u).astype(BF16)
        part = jnp.dot(a, wd_ref[f0:f0 + FF_TILE, :], preferred_element_type=F32)
        acc = part if acc is None else acc + part
    return acc


def _pre_kernel(x_ref, ct_ref, st_ref, c_ref, s_ref, n1_ref, wg_ref, wu_ref, wd_ref, nmix_ref,
                win_ref, qn_ref, wqat_ref, wqbt_ref, kvn_ref, wuk_ref, wuvt_ref,
                h_ref, qt_ref, k_ref, vt_ref, uh_ref):
    x = x_ref[0]
    h = x + 0.5 * _swiglu(_rms(x, n1_ref[...]).astype(BF16), wg_ref, wu_ref, wd_ref)
    h_ref[0] = h
    hn = _rms(h, nmix_ref[...]).astype(BF16)
    u = jnp.dot(hn, win_ref[...], preferred_element_type=F32)
    o_kv = Q_LORA
    o_ka = o_kv + KV_LORA
    o_kb = o_ka + HEAD_PAD
    o_h = o_kb + HEAD_PAD
    uh_ref[0] = u[:, o_h:]

    cqn = _rms(u[:, :o_kv], qn_ref[...]).astype(BF16)
    qat = lax.dot_general(wqat_ref[...], cqn, NT_DIMS, preferred_element_type=F32)
    qbt = lax.dot_general(wqbt_ref[...], cqn, NT_DIMS, preferred_element_type=F32)
    ct = ct_ref[...]
    st = st_ref[...]
    scale = (QK_NOPE + QK_ROPE) ** -0.5 * LOG2_E
    for hd in range(MLA_HEADS):
        r0 = hd * HEAD_PAD
        qh = (qat[r0:r0 + HEAD_PAD] * ct + qbt[r0:r0 + HEAD_PAD] * st) * scale
        qt_ref[0, r0:r0 + HEAD_PAD, :] = qh.astype(BF16)

    ckvn = _rms(u[:, o_kv:o_ka], kvn_ref[...]).astype(BF16)
    kn = jnp.dot(ckvn, wuk_ref[...], preferred_element_type=F32)
    kr = u[:, o_ka:o_kb] * c_ref[...] + u[:, o_kb:o_h] * s_ref[...]
    for hd in range(MLA_HEADS):
        r0 = hd * HEAD_PAD
        k_ref[0, :, r0:r0 + HEAD_PAD] = (kn[:, r0:r0 + HEAD_PAD] + kr).astype(BF16)

    vt = lax.dot_general(wuvt_ref[...], ckvn, NT_DIMS, preferred_element_type=F32)
    for hd in range(MLA_HEADS):
        vt_ref[0, hd, 0] = vt[hd * V_HEAD:(hd + 1) * V_HEAD].astype(BF16)


def _pre_call(x, tabs, wts, tm):
    B, S, D = x.shape
    nt = S // tm
    ct, st, c, s = tabs
    n_h = wts["w_in"].shape[1] - (Q_LORA + KV_LORA + 2 * HEAD_PAD)
    vm = pl.BlockSpec(memory_space=pltpu.VMEM)
    in_specs = [
        pl.BlockSpec((1, tm, D), lambda b, i: (b, i, 0)),
        pl.BlockSpec((HEAD_PAD, tm), lambda b, i: (0, i)),
        pl.BlockSpec((HEAD_PAD, tm), lambda b, i: (0, i)),
        pl.BlockSpec((tm, HEAD_PAD), lambda b, i: (i, 0)),
        pl.BlockSpec((tm, HEAD_PAD), lambda b, i: (i, 0)),
    ] + [vm] * 12
    out_shape = [
        jax.ShapeDtypeStruct((B, S, D), F32),
        jax.ShapeDtypeStruct((B, MLA_HEADS * HEAD_PAD, S), BF16),
        jax.ShapeDtypeStruct((B, S, MLA_HEADS * HEAD_PAD), BF16),
        jax.ShapeDtypeStruct((B, MLA_HEADS, nt, V_HEAD, tm), BF16),
        jax.ShapeDtypeStruct((B, S, n_h), F32),
    ]
    out_specs = [
        pl.BlockSpec((1, tm, D), lambda b, i: (b, i, 0)),
        pl.BlockSpec((1, MLA_HEADS * HEAD_PAD, tm), lambda b, i: (b, 0, i)),
        pl.BlockSpec((1, tm, MLA_HEADS * HEAD_PAD), lambda b, i: (b, i, 0)),
        pl.BlockSpec((1, MLA_HEADS, 1, V_HEAD, tm), lambda b, i: (b, 0, i, 0, 0)),
        pl.BlockSpec((1, tm, n_h), lambda b, i: (b, i, 0)),
    ]
    return pl.pallas_call(
        _pre_kernel,
        grid=(B, nt),
        in_specs=in_specs,
        out_specs=out_specs,
        out_shape=out_shape,
        compiler_params=pltpu.CompilerParams(
            dimension_semantics=("arbitrary", "arbitrary"), vmem_limit_bytes=VMEM_LIMIT),
        name="pre",
    )(x, ct, st, c, s, wts["ffn1_norm"], wts["ffn1_wg"], wts["ffn1_wu"], wts["ffn1_wd"],
      wts["mix_norm"], wts["w_in"], wts["q_norm"], wts["w_qat"], wts["w_qbt"], wts["kv_norm"],
      wts["w_ukp"], wts["w_uvt"])


def _attn_kernel(qt_ref, k_ref, vt_ref, o_ref, s_ref, *, tk, vt_w):
    q = qt_ref[0]
    tq = q.shape[1]
    n_kv = k_ref.shape[1] // tk
    n_sub = tk // vt_w
    unroll = math.gcd(n_kv, KV_UNROLL)

    def scores(i):
        k = k_ref[0, pl.ds(pl.multiple_of(i * tk, tk), tk), :]
        return jnp.dot(k, q, preferred_element_type=F32)

    s_ref[...] = scores(0)

    def body(j, carry):
        m, l, acc = carry
        s = s_ref[...]
        for u in range(unroll):
            i = j * unroll + u
            s_next = scores(jnp.minimum(i + 1, n_kv - 1))
            m_new = jnp.maximum(m, jnp.max(s, axis=0, keepdims=True))
            alpha = jnp.exp2(m - m_new)
            p = jnp.exp2(s - m_new)
            l = alpha * l + jnp.sum(p, axis=0, keepdims=True)
            pb = p.astype(BF16)
            acc = alpha * acc
            for t in range(n_sub):
                acc = acc + jnp.dot(vt_ref[0, 0, i * n_sub + t], pb[t * vt_w:(t + 1) * vt_w],
                                    preferred_element_type=F32)
            m = m_new
            s = s_next
        s_ref[...] = s
        return m, l, acc

    m0 = jnp.full((1, tq), -1e30, F32)
    l0 = jnp.zeros((1, tq), F32)
    a0 = jnp.zeros((V_HEAD, tq), F32)
    _, l, acc = lax.fori_loop(0, n_kv // unroll, body, (m0, l0, a0))
    o_ref[0] = acc * (1.0 / l)


def _attn_call(qt, k, vt, tq, tk):
    B, _, S = qt.shape
    vt_w = vt.shape[-1]
    n_vt = vt.shape[2]
    return pl.pallas_call(
        functools.partial(_attn_kernel, tk=tk, vt_w=vt_w),
        grid=(B, MLA_HEADS, S // tq),
        in_specs=[
            pl.BlockSpec((1, HEAD_PAD, tq), lambda b, h, i: (b, h, i)),
            pl.BlockSpec((1, S, HEAD_PAD), lambda b, h, i: (b, 0, h)),
            pl.BlockSpec((1, 1, n_vt, V_HEAD, vt_w), lambda b, h, i: (b, h, 0, 0, 0)),
        ],
        out_specs=pl.BlockSpec((1, V_HEAD, tq), lambda b, h, i: (b, h, i)),
        out_shape=jax.ShapeDtypeStruct((B, MLA_WIDTH, S), F32),
        scratch_shapes=[pltpu.VMEM((tk, tq), F32)],
        compiler_params=pltpu.CompilerParams(
            dimension_semantics=("arbitrary", "arbitrary", "arbitrary"),
            vmem_limit_bytes=VMEM_LIMIT),
        name="attn",
    )(qt, k, vt)


def _hgrn_levels(chunk):
    return [chunk >> (j + 1) for j in range(int(math.log2(chunk)))]


def _hgrn_consts(chunk):
    t = np.arange(chunk)
    mats = [(t[None, :] <= t[:, None]).astype(np.float32)]
    for m in _hgrn_levels(chunk):
        ref = (t // (2 * m)) * 2 * m + m - 1
        later = (t % (2 * m)) >= m
        u = t[None, :]
        d = np.where(later[:, None], (u > ref[:, None]) & (u <= t[:, None]),
                     (u > t[:, None]) & (u <= ref[:, None]))
        mats.append(d.astype(np.float32))
    mats.append(np.ones((8, chunk), np.float32))
    fwd = np.concatenate(mats, axis=0)
    n_blk = len(mats) - 1
    bwd = np.concatenate([np.flip(mm, (0, 1)) for mm in mats[:n_blk]] + [mats[-1]], axis=0)
    return np.stack([fwd, bwd])


def _hgrn_kernel(q_ref, v_ref, z_ref, lb_ref, d_ref, o_ref, st_ref, *, chunk):
    direction = pl.program_id(0)

    @pl.when(pl.program_id(3) == 0)
    def _():
        st_ref[...] = jnp.zeros_like(st_ref)

    levels = _hgrn_levels(chunk)
    lb = lb_ref[0]
    qr = q_ref[0]
    v = v_ref[0]
    z = z_ref[0]
    q = qr * _sigmoid(qr)
    k = (1.0 - lb) * _sigmoid(-z)
    g = jnp.log(lb + (1.0 - lb) * _sigmoid(z))

    g_hi = g.astype(BF16)
    r1 = g - g_hi.astype(F32)
    g_mid = r1.astype(BF16)
    g_lo = (r1 - g_mid.astype(F32)).astype(BF16)
    g3 = jnp.concatenate([g_hi, g_mid, g_lo], axis=1)
    e3 = jnp.dot(d_ref[0], g3, preferred_element_type=F32)
    e = e3[:, :HG_DK] + (e3[:, HG_DK:2 * HG_DK] + e3[:, 2 * HG_DK:])
    b = e[:chunk]
    tot = e[(len(levels) + 1) * chunk:(len(levels) + 1) * chunk + 1]

    st = st_ref[...]
    vb = v.astype(BF16)
    o = lax.dot_general((q * jnp.exp(b)).astype(BF16), st.astype(BF16), NT_DIMS,
                        preferred_element_type=F32)

    row = lax.broadcasted_iota(jnp.int32, (chunk, chunk), 0)
    col = lax.broadcasted_iota(jnp.int32, (chunk, chunk), 1)
    x = row ^ col
    earlier = (row - col) * (1 - 2 * direction) > 0
    a = jnp.where(x == 0, jnp.sum(q * k, axis=-1, keepdims=True), 0.0)
    for li, m in enumerate(levels):
        w = jnp.exp(e[(li + 1) * chunk:(li + 2) * chunk])
        al = lax.dot_general((q * w).astype(BF16), (k * w).astype(BF16), NT_DIMS,
                             preferred_element_type=F32)
        a = a + jnp.where(earlier & (x >= m) & (x < 2 * m), al, 0.0)
    o = o + jnp.dot(a.astype(BF16), vb, preferred_element_type=F32)
    o_ref[0, 0] = o

    kd = (k * jnp.exp(jnp.minimum(tot - b, 0.0))).astype(BF16)
    st_ref[...] = st * jnp.exp(tot) + lax.dot_general(vb, kd, TN_DIMS, preferred_element_type=F32)


def _hgrn_call(uh, lb3, dmat, chunk):
    B, S, _ = uh.shape
    n = S // chunk
    n_lb = lb3.shape[0] // 2

    def t_idx(d, i):
        return i + d * (n - 1 - 2 * i)

    return pl.pallas_call(
        functools.partial(_hgrn_kernel, chunk=chunk),
        grid=(2, B, HG_HEADS, n),
        in_specs=[
            pl.BlockSpec((1, chunk, HG_DK), lambda d, b, h, i: (b, t_idx(d, i), h)),
            pl.BlockSpec((1, chunk, HG_DV), lambda d, b, h, i: (b, t_idx(d, i), HG_HEADS + h)),
            pl.BlockSpec((1, chunk, HG_DK), lambda d, b, h, i: (b, t_idx(d, i), (2 + d) * HG_HEADS + h)),
            pl.BlockSpec((1, 1, HG_DK), lambda d, b, h, i: (d * n_lb, 0, h)),
            pl.BlockSpec((1,) + dmat.shape[1:], lambda d, b, h, i: (d, 0, 0)),
        ],
        out_specs=pl.BlockSpec((1, 1, chunk, HG_DV), lambda d, b, h, i: (d, b, t_idx(d, i), h)),
        out_shape=jax.ShapeDtypeStruct((2, B, S, HG_WIDTH), F32),
        scratch_shapes=[pltpu.VMEM((HG_DV, HG_DK), F32)],
        compiler_params=pltpu.CompilerParams(
            dimension_semantics=("arbitrary", "arbitrary", "arbitrary", "arbitrary"),
            vmem_limit_bytes=VMEM_LIMIT),
        name="hgrn",
    )(uh, uh, uh, lb3, dmat)


def _post_kernel(h1_ref, ot_ref, of_ref, ob_ref, hg_ref, p_ref, hgn_ref, wot_ref, wob_ref,
                 n2_ref, wg_ref, wu_ref, wd_ref, npl_ref, wpg_ref, wpp_ref, nf_ref, y_ref):
    o = of_ref[0, 0] + ob_ref[0, 0]
    gt = hg_ref[0]
    parts = []
    for hd in range(HG_HEADS):
        oh = o[:, hd * HG_DV:(hd + 1) * HG_DV]
        parts.append(_rms(oh, hgn_ref[:, hd * HG_DV:(hd + 1) * HG_DV]))
    hmix = (jnp.concatenate(parts, axis=1) * (gt * _sigmoid(gt))).astype(BF16)
    h = (h1_ref[0]
         + lax.dot_general(ot_ref[0].astype(BF16), wot_ref[...], TN_DIMS, preferred_element_type=F32)
         + jnp.dot(hmix, wob_ref[...], preferred_element_type=F32))
    h = h + 0.5 * _swiglu(_rms(h, n2_ref[...]).astype(BF16), wg_ref, wu_ref, wd_ref)
    gate = _sigmoid(jnp.dot(_rms(h, npl_ref[...]).astype(BF16), wpg_ref[...],
                            preferred_element_type=F32))
    h = h + gate * jnp.dot(p_ref[0, 0].astype(BF16), wpp_ref[...], preferred_element_type=F32)
    y_ref[0] = _rms(h, nf_ref[...])


def _post_call(h1, ot, o2, uh, p, layer, wts, tm):
    B, S, D = h1.shape
    g_blk = (uh.shape[2] - HG_WIDTH) // HG_WIDTH
    vm = pl.BlockSpec(memory_space=pltpu.VMEM)
    in_specs = [
        pl.BlockSpec((1, tm, D), lambda b, i: (b, i, 0)),
        pl.BlockSpec((1, MLA_WIDTH, tm), lambda b, i: (b, 0, i)),
        pl.BlockSpec((1, 1, tm, HG_WIDTH), lambda b, i: (0, b, i, 0)),
        pl.BlockSpec((1, 1, tm, HG_WIDTH), lambda b, i: (1, b, i, 0)),
        pl.BlockSpec((1, tm, HG_WIDTH), lambda b, i: (b, i, g_blk)),
        pl.BlockSpec((1, 1, tm, p.shape[-1]), lambda b, i: (layer, b, i, 0)),
    ] + [vm] * 11
    return pl.pallas_call(
        _post_kernel,
        grid=(B, S // tm),
        in_specs=in_specs,
        out_specs=pl.BlockSpec((1, tm, D), lambda b, i: (b, i, 0)),
        out_shape=jax.ShapeDtypeStruct((B, S, D), F32),
        compiler_params=pltpu.CompilerParams(
            dimension_semantics=("arbitrary", "arbitrary"), vmem_limit_bytes=VMEM_LIMIT),
        name="post",
    )(h1, ot, o2, o2, uh, p, wts["hg_norm"], wts["w_o_top"], wts["w_o_bot"], wts["ffn2_norm"],
      wts["ffn2_wg"], wts["ffn2_wu"], wts["ffn2_wd"], wts["ple_norm"], wts["w_ple_gate"],
      wts["w_ple_proj"], wts["final_norm"])


def _rope_tables(S):
    half = QK_ROPE // 2
    inv_freq = jnp.exp(jnp.arange(0, QK_ROPE, 2, dtype=F32) * (-math.log(ROPE_BASE) / QK_ROPE))
    ang = jnp.arange(S, dtype=F32)[:, None] * inv_freq[None, :]
    cos, sin = jnp.cos(ang), jnp.sin(ang)
    ones = jnp.ones((S, QK_NOPE), F32)
    zpad = jnp.zeros((S, HEAD_PAD - QK_NOPE - QK_ROPE), F32)
    c = jnp.concatenate([ones, cos, cos, zpad], axis=1)
    s = jnp.concatenate([0.0 * ones, -sin, sin, zpad], axis=1)
    assert half * 2 == QK_ROPE
    return c.T, s.T, c, s


def _layer_weights(l, ffn1_norm, ffn1_wg, ffn1_wu, ffn1_wd, mix_norm, w_in, q_norm, w_uq, kv_norm,
                   w_uk, w_uv, hg_norm, w_o, ffn2_norm, ffn2_wg, ffn2_wu, ffn2_wd, ple_norm,
                   w_ple_gate, w_ple_proj, final_norm):
    D = w_in.shape[1]
    half = QK_ROPE // 2
    o_kr = Q_LORA + KV_LORA
    wi = w_in[l]
    z64 = jnp.zeros((D, QK_NOPE), F32)
    z32 = jnp.zeros((D, HEAD_PAD - QK_NOPE - QK_ROPE), F32)
    x1, x2 = wi[:, o_kr:o_kr + half], wi[:, o_kr + half:o_kr + QK_ROPE]
    w_in_p = jnp.concatenate([wi[:, :o_kr], z64, x1, x2, z32, z64, x2, x1, z32,
                              wi[:, o_kr + QK_ROPE:]], axis=1)

    wq = w_uq[l].reshape(Q_LORA, MLA_HEADS, QK_NOPE + QK_ROPE)
    qn, q1, q2 = wq[..., :QK_NOPE], wq[..., QK_NOPE:QK_NOPE + half], wq[..., QK_NOPE + half:]
    zq = jnp.zeros((Q_LORA, MLA_HEADS, HEAD_PAD - QK_NOPE - QK_ROPE), F32)
    wqa = jnp.concatenate([qn, q1, q2, zq], axis=-1).reshape(Q_LORA, MLA_HEADS * HEAD_PAD)
    wqb = jnp.concatenate([0.0 * qn, q2, q1, zq], axis=-1).reshape(Q_LORA, MLA_HEADS * HEAD_PAD)

    wk = w_uk[l].reshape(KV_LORA, MLA_HEADS, QK_NOPE)
    wkp = jnp.concatenate([wk, jnp.zeros((KV_LORA, MLA_HEADS, HEAD_PAD - QK_NOPE), F32)],
                          axis=-1).reshape(KV_LORA, MLA_HEADS * HEAD_PAD)

    row = lambda a: a.reshape(1, -1)
    return {
        "ffn1_norm": row(ffn1_norm[l]), "ffn1_wg": ffn1_wg[l].astype(BF16),
        "ffn1_wu": ffn1_wu[l].astype(BF16), "ffn1_wd": ffn1_wd[l].astype(BF16),
        "mix_norm": row(mix_norm[l]), "w_in": w_in_p.astype(BF16),
        "q_norm": row(q_norm[l]), "w_qat": wqa.T.astype(BF16), "w_qbt": wqb.T.astype(BF16),
        "kv_norm": row(kv_norm[l]), "w_ukp": wkp.astype(BF16), "w_uvt": w_uv[l].T.astype(BF16),
        "hg_norm": row(hg_norm[l]),
        "w_o_top": w_o[l][:MLA_WIDTH].astype(BF16), "w_o_bot": w_o[l][MLA_WIDTH:].astype(BF16),
        "ffn2_norm": row(ffn2_norm[l]), "ffn2_wg": ffn2_wg[l].astype(BF16),
        "ffn2_wu": ffn2_wu[l].astype(BF16), "ffn2_wd": ffn2_wd[l].astype(BF16),
        "ple_norm": row(ple_norm[l]), "w_ple_gate": w_ple_gate[l].astype(BF16),
        "w_ple_proj": w_ple_proj[l].astype(BF16), "final_norm": row(final_norm),
    }


def _tiles(S):
    tm = min(256, S)
    tq = min(256, S)
    tk = min(512, S)
    chunk = min(64, S)
    return tm, tq, tk, chunk


def _trunk(x, p, layers, lb3, depth):
    B, S, D = x.shape
    tm, tq, tk, chunk = _tiles(S)
    tabs = _rope_tables(S)
    dmat = jnp.asarray(_hgrn_consts(chunk), BF16)
    h = x
    for l in range(depth):
        wts = layers[l]
        h1, qt, k, vt, uh = _pre_call(h, tabs, wts, tm)
        ot = _attn_call(qt, k, vt, tq, tk)
        o2 = _hgrn_call(uh, lb3[l], dmat, chunk)
        h = _post_call(h1, ot, o2, uh, p, l, wts, tm)
    return h


def kernel(x_prompt, x_sample, p_prompt, p_sample, ffn1_norm, ffn1_wg, ffn1_wu, ffn1_wd, mix_norm,
           w_in, q_norm, w_uq, kv_norm, w_uk, w_uv, hg_lb, hg_norm, w_o, ffn2_norm, ffn2_wg,
           ffn2_wu, ffn2_wd, ple_norm, w_ple_gate, w_ple_proj, final_norm):
    depth = w_in.shape[0]
    assert depth == 1, "the final norm is fused into the last layer's post kernel"
    layers = [
        _layer_weights(l, ffn1_norm, ffn1_wg, ffn1_wu, ffn1_wd, mix_norm, w_in, q_norm, w_uq,
                       kv_norm, w_uk, w_uv, hg_norm, w_o, ffn2_norm, ffn2_wg, ffn2_wu, ffn2_wd,
                       ple_norm, w_ple_gate, w_ple_proj, final_norm)
        for l in range(depth)
    ]
    lb = jnp.cumsum(jax.nn.softmax(hg_lb.astype(F32), axis=1), axis=1)
    lb3 = [jnp.stack([lb[0, l], lb[1, l]]).reshape(2, 1, HG_WIDTH) for l in range(depth)]
    y_prompt = _trunk(x_prompt, p_prompt, layers, lb3, depth)
    y_sample = _trunk(x_sample, p_sample, layers, lb3, depth)
    return (y_prompt, y_sample)
```

```python
import functools
import math

import numpy as np
import jax
import jax.numpy as jnp
from jax import lax
from jax.experimental import pallas as pl
from jax.experimental.pallas import tpu as pltpu

F32 = jnp.float32
BF16 = jnp.bfloat16

EPS = 1e-6
MLA_HEADS = 8
QK_NOPE = 64
QK_ROPE = 32
V_HEAD = 64
Q_LORA = 384
KV_LORA = 256
ROPE_BASE = 10000.0
HG_HEADS = 4
HG_DK = 128
HG_DV = 128
HG_WIDTH = HG_HEADS * HG_DV
MLA_WIDTH = MLA_HEADS * V_HEAD

LANES = 128
HEAD_PAD = LANES
FF_TILE = 256
VMEM_LIMIT = 58 * 1024 * 1024
LOG2_E = 1.4426950408889634
KV_UNROLL = 8

NT_DIMS = (((1,), (1,)), ((), ()))
TN_DIMS = (((0,), (0,)), ((), ()))


def _rms(x, g):
    return x * lax.rsqrt(jnp.mean(x * x, axis=-1, keepdims=True) + EPS) * g


def _sigmoid(x):
    return 1.0 / (1.0 + jnp.exp(-x))


def _swiglu(xn, wg_ref, wu_ref, wd_ref):
    d_ff = wg_ref.shape[1]
    acc = None
    for f0 in range(0, d_ff, FF_TILE):
        g = jnp.dot(xn, wg_ref[:, f0:f0 + FF_TILE], preferred_element_type=F32)
        u = jnp.dot(xn, wu_ref[:, f0:f0 + FF_TILE], preferred_element_type=F32)
        a = (g * _sigmoid(g) * u).astype(BF16)
        part = jnp.dot(a, wd_ref[f0:f0 + FF_TILE, :], preferred_element_type=F32)
        acc = part if acc is None else acc + part
    return acc


def _pre_kernel(x_ref, ct_ref, st_ref, c_ref, s_ref, n1_ref, wg_ref, wu_ref, wd_ref, nmix_ref,
                win_ref, qn_ref, wqat_ref, wqbt_ref, kvn_ref, wuk_ref, wuvt_ref,
                h_ref, qt_ref, k_ref, vt_ref, uh_ref):
    x = x_ref[0]
    h = x + 0.5 * _swiglu(_rms(x, n1_ref[...]).astype(BF16), wg_ref, wu_ref, wd_ref)
    h_ref[0] = h
    hn = _rms(h, nmix_ref[...]).astype(BF16)
    u = jnp.dot(hn, win_ref[...], preferred_element_type=F32)
    o_kv = Q_LORA
    o_ka = o_kv + KV_LORA
    o_kb = o_ka + HEAD_PAD
    o_h = o_kb + HEAD_PAD
    uh_ref[0] = u[:, o_h:]

    cqn = _rms(u[:, :o_kv], qn_ref[...]).astype(BF16)
    qat = lax.dot_general(wqat_ref[...], cqn, NT_DIMS, preferred_element_type=F32)
    qbt = lax.dot_general(wqbt_ref[...], cqn, NT_DIMS, preferred_element_type=F32)
    ct = ct_ref[...]
    st = st_ref[...]
    scale = (QK_NOPE + QK_ROPE) ** -0.5 * LOG2_E
    for hd in range(MLA_HEADS):
        r0 = hd * HEAD_PAD
        qh = (qat[r0:r0 + HEAD_PAD] * ct + qbt[r0:r0 + HEAD_PAD] * st) * scale
        qt_ref[0, r0:r0 + HEAD_PAD, :] = qh.astype(BF16)

    ckvn = _rms(u[:, o_kv:o_ka], kvn_ref[...]).astype(BF16)
    kn = jnp.dot(ckvn, wuk_ref[...], preferred_element_type=F32)
    kr = u[:, o_ka:o_kb] * c_ref[...] + u[:, o_kb:o_h] * s_ref[...]
    for hd in range(MLA_HEADS):
        r0 = hd * HEAD_PAD
        k_ref[0, :, r0:r0 + HEAD_PAD] = (kn[:, r0:r0 + HEAD_PAD] + kr).astype(BF16)

    vt = lax.dot_general(wuvt_ref[...], ckvn, NT_DIMS, preferred_element_type=F32)
    for hd in range(MLA_HEADS):
        vt_ref[0, hd, 0] = vt[hd * V_HEAD:(hd + 1) * V_HEAD].astype(BF16)


def _pre_call(x, tabs, wts, tm):
    B, S, D = x.shape
    nt = S // tm
    ct, st, c, s = tabs
    n_h = wts["w_in"].shape[1] - (Q_LORA + KV_LORA + 2 * HEAD_PAD)
    vm = pl.BlockSpec(memory_space=pltpu.VMEM)
    in_specs = [
        pl.BlockSpec((1, tm, D), lambda b, i: (b, i, 0)),
        pl.BlockSpec((HEAD_PAD, tm), lambda b, i: (0, i)),
        pl.BlockSpec((HEAD_PAD, tm), lambda b, i: (0, i)),
        pl.BlockSpec((tm, HEAD_PAD), lambda b, i: (i, 0)),
        pl.BlockSpec((tm, HEAD_PAD), lambda b, i: (i, 0)),
    ] + [vm] * 12
    out_shape = [
        jax.ShapeDtypeStruct((B, S, D), F32),
        jax.ShapeDtypeStruct((B, MLA_HEADS * HEAD_PAD, S), BF16),
        jax.ShapeDtypeStruct((B, S, MLA_HEADS * HEAD_PAD), BF16),
        jax.ShapeDtypeStruct((B, MLA_HEADS, nt, V_HEAD, tm), BF16),
        jax.ShapeDtypeStruct((B, S, n_h), F32),
    ]
    out_specs = [
        pl.BlockSpec((1, tm, D), lambda b, i: (b, i, 0)),
        pl.BlockSpec((1, MLA_HEADS * HEAD_PAD, tm), lambda b, i: (b, 0, i)),
        pl.BlockSpec((1, tm, MLA_HEADS * HEAD_PAD), lambda b, i: (b, i, 0)),
        pl.BlockSpec((1, MLA_HEADS, 1, V_HEAD, tm), lambda b, i: (b, 0, i, 0, 0)),
        pl.BlockSpec((1, tm, n_h), lambda b, i: (b, i, 0)),
    ]
    return pl.pallas_call(
        _pre_kernel,
        grid=(B, nt),
        in_specs=in_specs,
        out_specs=out_specs,
        out_shape=out_shape,
        compiler_params=pltpu.CompilerParams(
            dimension_semantics=("arbitrary", "arbitrary"), vmem_limit_bytes=VMEM_LIMIT),
        name="pre",
    )(x, ct, st, c, s, wts["ffn1_norm"], wts["ffn1_wg"], wts["ffn1_wu"], wts["ffn1_wd"],
      wts["mix_norm"], wts["w_in"], wts["q_norm"], wts["w_qat"], wts["w_qbt"], wts["kv_norm"],
      wts["w_ukp"], wts["w_uvt"])


def _attn_kernel(qt_ref, k_ref, vt_ref, o_ref, s_ref, *, tk, vt_w):
    q = qt_ref[0]
    tq = q.shape[1]
    n_kv = k_ref.shape[1] // tk
    n_sub = tk // vt_w
    unroll = math.gcd(n_kv, KV_UNROLL)

    def scores(i):
        k = k_ref[0, pl.ds(pl.multiple_of(i * tk, tk), tk), :]
        return jnp.dot(k, q, preferred_element_type=F32)

    s_ref[...] = scores(0)

    def body(j, carry):
        m, l, acc = carry
        s = s_ref[...]
        for u in range(unroll):
            i = j * unroll + u
            s_next = scores(jnp.minimum(i + 1, n_kv - 1))
            m_new = jnp.maximum(m, jnp.max(s, axis=0, keepdims=True))
            alpha = jnp.exp2(m - m_new)
            p = jnp.exp2(s - m_new)
            l = alpha * l + jnp.sum(p, axis=0, keepdims=True)
            pb = p.astype(BF16)
            acc = alpha * acc
            for t in range(n_sub):
                acc = acc + jnp.dot(vt_ref[0, 0, i * n_sub + t], pb[t * vt_w:(t + 1) * vt_w],
                                    preferred_element_type=F32)
            m = m_new
            s = s_next
        s_ref[...] = s
        return m, l, acc

    m0 = jnp.full((1, tq), -1e30, F32)
    l0 = jnp.zeros((1, tq), F32)
    a0 = jnp.zeros((V_HEAD, tq), F32)
    _, l, acc = lax.fori_loop(0, n_kv // unroll, body, (m0, l0, a0))
    o_ref[0] = acc * (1.0 / l)


def _attn_call(qt, k, vt, tq, tk):
    B, _, S = qt.shape
    vt_w = vt.shape[-1]
    n_vt = vt.shape[2]
    return pl.pallas_call(
        functools.partial(_attn_kernel, tk=tk, vt_w=vt_w),
        grid=(B, MLA_HEADS, S // tq),
        in_specs=[
            pl.BlockSpec((1, HEAD_PAD, tq), lambda b, h, i: (b, h, i)),
            pl.BlockSpec((1, S, HEAD_PAD), lambda b, h, i: (b, 0, h)),
            pl.BlockSpec((1, 1, n_vt, V_HEAD, vt_w), lambda b, h, i: (b, h, 0, 0, 0)),
        ],
        out_specs=pl.BlockSpec((1, V_HEAD, tq), lambda b, h, i: (b, h, i)),
        out_shape=jax.ShapeDtypeStruct((B, MLA_WIDTH, S), F32),
        scratch_shapes=[pltpu.VMEM((tk, tq), F32)],
        compiler_params=pltpu.CompilerParams(
            dimension_semantics=("arbitrary", "arbitrary", "arbitrary"),
            vmem_limit_bytes=VMEM_LIMIT),
        name="attn",
    )(qt, k, vt)


def _hgrn_levels(chunk):
    return [chunk >> (j + 1) for j in range(int(math.log2(chunk)))]


def _hgrn_consts(chunk):
    t = np.arange(chunk)
    mats = [(t[None, :] <= t[:, None]).astype(np.float32)]
    for m in _hgrn_levels(chunk):
        ref = (t // (2 * m)) * 2 * m + m - 1
        later = (t % (2 * m)) >= m
        u = t[None, :]
        d = np.where(later[:, None], (u > ref[:, None]) & (u <= t[:, None]),
                     (u > t[:, None]) & (u <= ref[:, None]))
        mats.append(d.astype(np.float32))
    mats.append(np.ones((8, chunk), np.float32))
    fwd = np.concatenate(mats, axis=0)
    n_blk = len(mats) - 1
    bwd = np.concatenate([np.flip(mm, (0, 1)) for mm in mats[:n_blk]] + [mats[-1]], axis=0)
    return np.stack([fwd, bwd])


def _hgrn_kernel(q_ref, v_ref, z_ref, lb_ref, d_ref, o_ref, st_ref, *, chunk):
    direction = pl.program_id(0)

    @pl.when(pl.program_id(2) == 0)
    def _():
        st_ref[...] = jnp.zeros_like(st_ref)

    levels = _hgrn_levels(chunk)
    lb = lb_ref[0]
    qr = q_ref[0]
    v_all = v_ref[0]
    z = z_ref[0]
    q_all = qr * _sigmoid(qr)
    k_all = (1.0 - lb) * _sigmoid(-z)
    g = jnp.log(lb + (1.0 - lb) * _sigmoid(z))

    g_hi = g.astype(BF16)
    r1 = g - g_hi.astype(F32)
    g_mid = r1.astype(BF16)
    g_lo = (r1 - g_mid.astype(F32)).astype(BF16)
    g3 = jnp.concatenate([g_hi, g_mid, g_lo], axis=1)
    e3 = jnp.dot(d_ref[0], g3, preferred_element_type=F32)
    e_all = e3[:, :HG_WIDTH] + (e3[:, HG_WIDTH:2 * HG_WIDTH] + e3[:, 2 * HG_WIDTH:])

    row = lax.broadcasted_iota(jnp.int32, (chunk, chunk), 0)
    col = lax.broadcasted_iota(jnp.int32, (chunk, chunk), 1)
    x = row ^ col
    earlier = (row - col) * (1 - 2 * direction) > 0
    n_blk = len(levels) + 1

    for hd in range(HG_HEADS):
        ln = slice(hd * HG_DK, (hd + 1) * HG_DK)
        q, k, v, e = q_all[:, ln], k_all[:, ln], v_all[:, ln], e_all[:, ln]
        b = e[:chunk]
        tot = e[n_blk * chunk:n_blk * chunk + 1]
        st = st_ref[hd]
        vb = v.astype(BF16)
        o = lax.dot_general((q * jnp.exp(b)).astype(BF16), st.astype(BF16), NT_DIMS,
                            preferred_element_type=F32)
        a = jnp.where(x == 0, jnp.sum(q * k, axis=-1, keepdims=True), 0.0)
        for li, m in enumerate(levels):
            w = jnp.exp(e[(li + 1) * chunk:(li + 2) * chunk])
            al = lax.dot_general((q * w).astype(BF16), (k * w).astype(BF16), NT_DIMS,
                                 preferred_element_type=F32)
            a = a + jnp.where(earlier & (x >= m) & (x < 2 * m), al, 0.0)
        o_ref[0, 0, :, ln] = o + jnp.dot(a.astype(BF16), vb, preferred_element_type=F32)
        kd = (k * jnp.exp(jnp.minimum(tot - b, 0.0))).astype(BF16)
        st_ref[hd] = st * jnp.exp(tot) + lax.dot_general(vb, kd, TN_DIMS,
                                                         preferred_element_type=F32)


def _hgrn_call(uh, lb3, dmat, chunk):
    B, S, _ = uh.shape
    n = S // chunk

    def t_idx(d, i):
        return i + d * (n - 1 - 2 * i)

    return pl.pallas_call(
        functools.partial(_hgrn_kernel, chunk=chunk),
        grid=(2, B, n),
        in_specs=[
            pl.BlockSpec((1, chunk, HG_WIDTH), lambda d, b, i: (b, t_idx(d, i), 0)),
            pl.BlockSpec((1, chunk, HG_WIDTH), lambda d, b, i: (b, t_idx(d, i), 1)),
            pl.BlockSpec((1, chunk, HG_WIDTH), lambda d, b, i: (b, t_idx(d, i), 2 + d)),
            pl.BlockSpec((1, 1, HG_WIDTH), lambda d, b, i: (d, 0, 0)),
            pl.BlockSpec((1,) + dmat.shape[1:], lambda d, b, i: (d, 0, 0)),
        ],
        out_specs=pl.BlockSpec((1, 1, chunk, HG_WIDTH), lambda d, b, i: (d, b, t_idx(d, i), 0)),
        out_shape=jax.ShapeDtypeStruct((2, B, S, HG_WIDTH), F32),
        scratch_shapes=[pltpu.VMEM((HG_HEADS, HG_DV, HG_DK), F32)],
        compiler_params=pltpu.CompilerParams(
            dimension_semantics=("arbitrary", "arbitrary", "arbitrary"),
            vmem_limit_bytes=VMEM_LIMIT),
        name="hgrn",
    )(uh, uh, uh, lb3, dmat)


def _post_kernel(h1_ref, ot_ref, of_ref, ob_ref, hg_ref, p_ref, hgn_ref, wot_ref, wob_ref,
                 n2_ref, wg_ref, wu_ref, wd_ref, npl_ref, wpg_ref, wpp_ref, nf_ref, y_ref):
    o = of_ref[0, 0] + ob_ref[0, 0]
    gt = hg_ref[0]
    parts = []
    for hd in range(HG_HEADS):
        oh = o[:, hd * HG_DV:(hd + 1) * HG_DV]
        parts.append(_rms(oh, hgn_ref[:, hd * HG_DV:(hd + 1) * HG_DV]))
    hmix = (jnp.concatenate(parts, axis=1) * (gt * _sigmoid(gt))).astype(BF16)
    h = (h1_ref[0]
         + lax.dot_general(ot_ref[0].astype(BF16), wot_ref[...], TN_DIMS, preferred_element_type=F32)
         + jnp.dot(hmix, wob_ref[...], preferred_element_type=F32))
    h = h + 0.5 * _swiglu(_rms(h, n2_ref[...]).astype(BF16), wg_ref, wu_ref, wd_ref)
    gate = _sigmoid(jnp.dot(_rms(h, npl_ref[...]).astype(BF16), wpg_ref[...],
                            preferred_element_type=F32))
    h = h + gate * jnp.dot(p_ref[0, 0].astype(BF16), wpp_ref[...], preferred_element_type=F32)
    y_ref[0] = _rms(h, nf_ref[...])


def _post_call(h1, ot, o2, uh, p, layer, wts, tm):
    B, S, D = h1.shape
    g_blk = (uh.shape[2] - HG_WIDTH) // HG_WIDTH
    vm = pl.BlockSpec(memory_space=pltpu.VMEM)
    in_specs = [
        pl.BlockSpec((1, tm, D), lambda b, i: (b, i, 0)),
        pl.BlockSpec((1, MLA_WIDTH, tm), lambda b, i: (b, 0, i)),
        pl.BlockSpec((1, 1, tm, HG_WIDTH), lambda b, i: (0, b, i, 0)),
        pl.BlockSpec((1, 1, tm, HG_WIDTH), lambda b, i: (1, b, i, 0)),
        pl.BlockSpec((1, tm, HG_WIDTH), lambda b, i: (b, i, g_blk)),
        pl.BlockSpec((1, 1, tm, p.shape[-1]), lambda b, i: (layer, b, i, 0)),
    ] + [vm] * 11
    return pl.pallas_call(
        _post_kernel,
        grid=(B, S // tm),
        in_specs=in_specs,
        out_specs=pl.BlockSpec((1, tm, D), lambda b, i: (b, i, 0)),
        out_shape=jax.ShapeDtypeStruct((B, S, D), F32),
        compiler_params=pltpu.CompilerParams(
            dimension_semantics=("arbitrary", "arbitrary"), vmem_limit_bytes=VMEM_LIMIT),
        name="post",
    )(h1, ot, o2, o2, uh, p, wts["hg_norm"], wts["w_o_top"], wts["w_o_bot"], wts["ffn2_norm"],
      wts["ffn2_wg"], wts["ffn2_wu"], wts["ffn2_wd"], wts["ple_norm"], wts["w_ple_gate"],
      wts["w_ple_proj"], wts["final_norm"])


def _rope_tables(S):
    half = QK_ROPE // 2
    inv_freq = jnp.exp(jnp.arange(0, QK_ROPE, 2, dtype=F32) * (-math.log(ROPE_BASE) / QK_ROPE))
    ang = jnp.arange(S, dtype=F32)[:, None] * inv_freq[None, :]
    cos, sin = jnp.cos(ang), jnp.sin(ang)
    ones = jnp.ones((S, QK_NOPE), F32)
    zpad = jnp.zeros((S, HEAD_PAD - QK_NOPE - QK_ROPE), F32)
    c = jnp.concatenate([ones, cos, cos, zpad], axis=1)
    s = jnp.concatenate([0.0 * ones, -sin, sin, zpad], axis=1)
    assert half * 2 == QK_ROPE
    return c.T, s.T, c, s


def _layer_weights(l, ffn1_norm, ffn1_wg, ffn1_wu, ffn1_wd, mix_norm, w_in, q_norm, w_uq, kv_norm,
                   w_uk, w_uv, hg_norm, w_o, ffn2_norm, ffn2_wg, ffn2_wu, ffn2_wd, ple_norm,
                   w_ple_gate, w_ple_proj, final_norm):
    D = w_in.shape[1]
    half = QK_ROPE // 2
    o_kr = Q_LORA + KV_LORA
    wi = w_in[l]
    z64 = jnp.zeros((D, QK_NOPE), F32)
    z32 = jnp.zeros((D, HEAD_PAD - QK_NOPE - QK_ROPE), F32)
    x1, x2 = wi[:, o_kr:o_kr + half], wi[:, o_kr + half:o_kr + QK_ROPE]
    w_in_p = jnp.concatenate([wi[:, :o_kr], z64, x1, x2, z32, z64, x2, x1, z32,
                              wi[:, o_kr + QK_ROPE:]], axis=1)

    wq = w_uq[l].reshape(Q_LORA, MLA_HEADS, QK_NOPE + QK_ROPE)
    qn, q1, q2 = wq[..., :QK_NOPE], wq[..., QK_NOPE:QK_NOPE + half], wq[..., QK_NOPE + half:]
    zq = jnp.zeros((Q_LORA, MLA_HEADS, HEAD_PAD - QK_NOPE - QK_ROPE), F32)
    wqa = jnp.concatenate([qn, q1, q2, zq], axis=-1).reshape(Q_LORA, MLA_HEADS * HEAD_PAD)
    wqb = jnp.concatenate([0.0 * qn, q2, q1, zq], axis=-1).reshape(Q_LORA, MLA_HEADS * HEAD_PAD)

    wk = w_uk[l].reshape(KV_LORA, MLA_HEADS, QK_NOPE)
    wkp = jnp.concatenate([wk, jnp.zeros((KV_LORA, MLA_HEADS, HEAD_PAD - QK_NOPE), F32)],
                          axis=-1).reshape(KV_LORA, MLA_HEADS * HEAD_PAD)

    row = lambda a: a.reshape(1, -1)
    return {
        "ffn1_norm": row(ffn1_norm[l]), "ffn1_wg": ffn1_wg[l].astype(BF16),
        "ffn1_wu": ffn1_wu[l].astype(BF16), "ffn1_wd": ffn1_wd[l].astype(BF16),
        "mix_norm": row(mix_norm[l]), "w_in": w_in_p.astype(BF16),
        "q_norm": row(q_norm[l]), "w_qat": wqa.T.astype(BF16), "w_qbt": wqb.T.astype(BF16),
        "kv_norm": row(kv_norm[l]), "w_ukp": wkp.astype(BF16), "w_uvt": w_uv[l].T.astype(BF16),
        "hg_norm": row(hg_norm[l]),
        "w_o_top": w_o[l][:MLA_WIDTH].astype(BF16), "w_o_bot": w_o[l][MLA_WIDTH:].astype(BF16),
        "ffn2_norm": row(ffn2_norm[l]), "ffn2_wg": ffn2_wg[l].astype(BF16),
        "ffn2_wu": ffn2_wu[l].astype(BF16), "ffn2_wd": ffn2_wd[l].astype(BF16),
        "ple_norm": row(ple_norm[l]), "w_ple_gate": w_ple_gate[l].astype(BF16),
        "w_ple_proj": w_ple_proj[l].astype(BF16), "final_norm": row(final_norm),
    }


def _tiles(S):
    tm = min(256, S)
    tq = min(256, S)
    tk = min(512, S)
    chunk = min(64, S)
    return tm, tq, tk, chunk


def _trunk(x, p, layers, lb3, depth):
    B, S, D = x.shape
    tm, tq, tk, chunk = _tiles(S)
    tabs = _rope_tables(S)
    dmat = jnp.asarray(_hgrn_consts(chunk), BF16)
    h = x
    for l in range(depth):
        wts = layers[l]
        h1, qt, k, vt, uh = _pre_call(h, tabs, wts, tm)
        ot = _attn_call(qt, k, vt, tq, tk)
        o2 = _hgrn_call(uh, lb3[l], dmat, chunk)
        h = _post_call(h1, ot, o2, uh, p, l, wts, tm)
    return h


def kernel(x_prompt, x_sample, p_prompt, p_sample, ffn1_norm, ffn1_wg, ffn1_wu, ffn1_wd, mix_norm,
           w_in, q_norm, w_uq, kv_norm, w_uk, w_uv, hg_lb, hg_norm, w_o, ffn2_norm, ffn2_wg,
           ffn2_wu, ffn2_wd, ple_norm, w_ple_gate, w_ple_proj, final_norm):
    depth = w_in.shape[0]
    assert depth == 1, "the final norm is fused into the last layer's post kernel"
    layers = [
        _layer_weights(l, ffn1_norm, ffn1_wg, ffn1_wu, ffn1_wd, mix_norm, w_in, q_norm, w_uq,
                       kv_norm, w_uk, w_uv, hg_norm, w_o, ffn2_norm, ffn2_wg, ffn2_wu, ffn2_wd,
                       ple_norm, w_ple_gate, w_ple_proj, final_norm)
        for l in range(depth)
    ]
    lb = jnp.cumsum(jax.nn.softmax(hg_lb.astype(F32), axis=1), axis=1)
    lb3 = [jnp.stack([lb[0, l], lb[1, l]]).reshape(2, 1, HG_WIDTH) for l in range(depth)]
    y_prompt = _trunk(x_prompt, p_prompt, layers, lb3, depth)
    y_sample = _trunk(x_sample, p_sample, layers, lb3, depth)
    return (y_prompt, y_sample)
```

```python
import functools
import math

import numpy as np
import jax
import jax.numpy as jnp
from jax import lax
from jax.experimental import pallas as pl
from jax.experimental.pallas import tpu as pltpu

F32 = jnp.float32
BF16 = jnp.bfloat16

EPS = 1e-6
MLA_HEADS = 8
QK_NOPE = 64
QK_ROPE = 32
V_HEAD = 64
Q_LORA = 384
KV_LORA = 256
ROPE_BASE = 10000.0
HG_HEADS = 4
HG_DK = 128
HG_DV = 128
HG_WIDTH = HG_HEADS * HG_DV
MLA_WIDTH = MLA_HEADS * V_HEAD

LANES = 128
HEAD_PAD = LANES
FF_TILE = 256
VMEM_LIMIT = 58 * 1024 * 1024
LOG2_E = 1.4426950408889634
KV_UNROLL = 8

NT_DIMS = (((1,), (1,)), ((), ()))
TN_DIMS = (((0,), (0,)), ((), ()))


def _rms(x, g):
    return x * lax.rsqrt(jnp.mean(x * x, axis=-1, keepdims=True) + EPS) * g


def _sigmoid(x):
    return 1.0 / (1.0 + jnp.exp(-x))


def _swiglu(xn, wg_ref, wu_ref, wd_ref):
    d_ff = wg_ref.shape[1]
    acc = None
    for f0 in range(0, d_ff, FF_TILE):
        g = jnp.dot(xn, wg_ref[:, f0:f0 + FF_TILE], preferred_element_type=F32)
        u = jnp.dot(xn, wu_ref[:, f0:f0 + FF_TILE], preferred_element_type=F32)
        a = (g * _sigmoid(g) * u).astype(BF16)
        part = jnp.dot(a, wd_ref[f0:f0 + FF_TILE, :], preferred_element_type=F32)
        acc = part if acc is None else acc + part
    return acc


def _pre_kernel(x_ref, ct_ref, st_ref, c_ref, s_ref, n1_ref, wg_ref, wu_ref, wd_ref, nmix_ref,
                win_ref, qn_ref, wqat_ref, wqbt_ref, kvn_ref, wuk_ref, wuvt_ref,
                h_ref, qt_ref, k_ref, vt_ref, uh_ref):
    x = x_ref[0]
    h = x + 0.5 * _swiglu(_rms(x, n1_ref[...]).astype(BF16), wg_ref, wu_ref, wd_ref)
    h_ref[0] = h
    hn = _rms(h, nmix_ref[...]).astype(BF16)
    u = jnp.dot(hn, win_ref[...], preferred_element_type=F32)
    o_kv = Q_LORA
    o_ka = o_kv + KV_LORA
    o_kb = o_ka + HEAD_PAD
    o_h = o_kb + HEAD_PAD
    uh_ref[0] = u[:, o_h:]

    cqn = _rms(u[:, :o_kv], qn_ref[...]).astype(BF16)
    qat = lax.dot_general(wqat_ref[...], cqn, NT_DIMS, preferred_element_type=F32)
    qbt = lax.dot_general(wqbt_ref[...], cqn, NT_DIMS, preferred_element_type=F32)
    ct = ct_ref[...]
    st = st_ref[...]
    scale = (QK_NOPE + QK_ROPE) ** -0.5 * LOG2_E
    for hd in range(MLA_HEADS):
        r0 = hd * HEAD_PAD
        qh = (qat[r0:r0 + HEAD_PAD] * ct + qbt[r0:r0 + HEAD_PAD] * st) * scale
        qt_ref[0, r0:r0 + HEAD_PAD, :] = qh.astype(BF16)

    ckvn = _rms(u[:, o_kv:o_ka], kvn_ref[...]).astype(BF16)
    kn = jnp.dot(ckvn, wuk_ref[...], preferred_element_type=F32)
    kr = u[:, o_ka:o_kb] * c_ref[...] + u[:, o_kb:o_h] * s_ref[...]
    for hd in range(MLA_HEADS):
        r0 = hd * HEAD_PAD
        k_ref[0, :, r0:r0 + HEAD_PAD] = (kn[:, r0:r0 + HEAD_PAD] + kr).astype(BF16)

    vt = lax.dot_general(wuvt_ref[...], ckvn, NT_DIMS, preferred_element_type=F32)
    for hd in range(MLA_HEADS):
        vt_ref[0, hd, 0] = vt[hd * V_HEAD:(hd + 1) * V_HEAD].astype(BF16)


def _pre_call(x, tabs, wts, tm):
    B, S, D = x.shape
    nt = S // tm
    ct, st, c, s = tabs
    n_h = wts["w_in"].shape[1] - (Q_LORA + KV_LORA + 2 * HEAD_PAD)
    vm = pl.BlockSpec(memory_space=pltpu.VMEM)
    in_specs = [
        pl.BlockSpec((1, tm, D), lambda b, i: (b, i, 0)),
        pl.BlockSpec((HEAD_PAD, tm), lambda b, i: (0, i)),
        pl.BlockSpec((HEAD_PAD, tm), lambda b, i: (0, i)),
        pl.BlockSpec((tm, HEAD_PAD), lambda b, i: (i, 0)),
        pl.BlockSpec((tm, HEAD_PAD), lambda b, i: (i, 0)),
    ] + [vm] * 12
    out_shape = [
        jax.ShapeDtypeStruct((B, S, D), F32),
        jax.ShapeDtypeStruct((B, MLA_HEADS * HEAD_PAD, S), BF16),
        jax.ShapeDtypeStruct((B, S, MLA_HEADS * HEAD_PAD), BF16),
        jax.ShapeDtypeStruct((B, MLA_HEADS, nt, V_HEAD, tm), BF16),
        jax.ShapeDtypeStruct((B, S, n_h), F32),
    ]
    out_specs = [
        pl.BlockSpec((1, tm, D), lambda b, i: (b, i, 0)),
        pl.BlockSpec((1, MLA_HEADS * HEAD_PAD, tm), lambda b, i: (b, 0, i)),
        pl.BlockSpec((1, tm, MLA_HEADS * HEAD_PAD), lambda b, i: (b, i, 0)),
        pl.BlockSpec((1, MLA_HEADS, 1, V_HEAD, tm), lambda b, i: (b, 0, i, 0, 0)),
        pl.BlockSpec((1, tm, n_h), lambda b, i: (b, i, 0)),
    ]
    return pl.pallas_call(
        _pre_kernel,
        grid=(B, nt),
        in_specs=in_specs,
        out_specs=out_specs,
        out_shape=out_shape,
        compiler_params=pltpu.CompilerParams(
            dimension_semantics=("arbitrary", "arbitrary"), vmem_limit_bytes=VMEM_LIMIT),
        name="pre",
    )(x, ct, st, c, s, wts["ffn1_norm"], wts["ffn1_wg"], wts["ffn1_wu"], wts["ffn1_wd"],
      wts["mix_norm"], wts["w_in"], wts["q_norm"], wts["w_qat"], wts["w_qbt"], wts["kv_norm"],
      wts["w_ukp"], wts["w_uvt"])


def _attn_kernel(qt_ref, k_ref, vt_ref, o_ref, s_ref, *, tk, vt_w):
    q = qt_ref[0]
    tq = q.shape[1]
    n_kv = k_ref.shape[1] // tk
    n_sub = tk // vt_w
    unroll = math.gcd(n_kv, KV_UNROLL)

    def scores(i):
        k = k_ref[0, pl.ds(pl.multiple_of(i * tk, tk), tk), :]
        return jnp.dot(k, q, preferred_element_type=F32)

    s_ref[...] = scores(0)

    def body(j, carry):
        m, l, acc = carry
        s = s_ref[...]
        for u in range(unroll):
            i = j * unroll + u
            s_next = scores(jnp.minimum(i + 1, n_kv - 1))
            m_new = jnp.maximum(m, jnp.max(s, axis=0, keepdims=True))
            alpha = jnp.exp2(m - m_new)
            p = jnp.exp2(s - m_new)
            l = alpha * l + jnp.sum(p, axis=0, keepdims=True)
            pb = p.astype(BF16)
            acc = alpha * acc
            for t in range(n_sub):
                acc = acc + jnp.dot(vt_ref[0, 0, i * n_sub + t], pb[t * vt_w:(t + 1) * vt_w],
                                    preferred_element_type=F32)
            m = m_new
            s = s_next
        s_ref[...] = s
        return m, l, acc

    m0 = jnp.full((1, tq), -1e30, F32)
    l0 = jnp.zeros((1, tq), F32)
    a0 = jnp.zeros((V_HEAD, tq), F32)
    _, l, acc = lax.fori_loop(0, n_kv // unroll, body, (m0, l0, a0))
    o_ref[0] = acc * (1.0 / l)


def _attn_call(qt, k, vt, tq, tk):
    B, _, S = qt.shape
    vt_w = vt.shape[-1]
    n_vt = vt.shape[2]
    return pl.pallas_call(
        functools.partial(_attn_kernel, tk=tk, vt_w=vt_w),
        grid=(B, MLA_HEADS, S // tq),
        in_specs=[
            pl.BlockSpec((1, HEAD_PAD, tq), lambda b, h, i: (b, h, i)),
            pl.BlockSpec((1, S, HEAD_PAD), lambda b, h, i: (b, 0, h)),
            pl.BlockSpec((1, 1, n_vt, V_HEAD, vt_w), lambda b, h, i: (b, h, 0, 0, 0)),
        ],
        out_specs=pl.BlockSpec((1, V_HEAD, tq), lambda b, h, i: (b, h, i)),
        out_shape=jax.ShapeDtypeStruct((B, MLA_WIDTH, S), F32),
        scratch_shapes=[pltpu.VMEM((tk, tq), F32)],
        compiler_params=pltpu.CompilerParams(
            dimension_semantics=("arbitrary", "arbitrary", "arbitrary"),
            vmem_limit_bytes=VMEM_LIMIT),
        name="attn",
    )(qt, k, vt)


def _hgrn_levels(chunk):
    return [chunk >> (j + 1) for j in range(int(math.log2(chunk)))]


def _hgrn_consts(chunk):
    t = np.arange(chunk)
    mats = [(t[None, :] <= t[:, None]).astype(np.float32)]
    for m in _hgrn_levels(chunk):
        ref = (t // (2 * m)) * 2 * m + m - 1
        later = (t % (2 * m)) >= m
        u = t[None, :]
        d = np.where(later[:, None], (u > ref[:, None]) & (u <= t[:, None]),
                     (u > t[:, None]) & (u <= ref[:, None]))
        mats.append(d.astype(np.float32))
    mats.append(np.ones((8, chunk), np.float32))
    fwd = np.concatenate(mats, axis=0)
    n_blk = len(mats) - 1
    bwd = np.concatenate([np.flip(mm, (0, 1)) for mm in mats[:n_blk]] + [mats[-1]], axis=0)
    return np.stack([fwd, bwd])


def _hgrn_kernel(q_ref, v_ref, z_ref, lb_ref, d_ref, o_ref, st_ref, *, chunk):
    direction = pl.program_id(0)

    @pl.when(pl.program_id(2) == 0)
    def _():
        st_ref[...] = jnp.zeros_like(st_ref)

    levels = _hgrn_levels(chunk)
    lb = lb_ref[0]
    qr = q_ref[0]
    v_all = v_ref[0]
    z = z_ref[0]
    q_all = qr * _sigmoid(qr)
    k_all = (1.0 - lb) * _sigmoid(-z)
    g = jnp.log(lb + (1.0 - lb) * _sigmoid(z))

    g_hi = g.astype(BF16)
    r1 = g - g_hi.astype(F32)
    g_mid = r1.astype(BF16)
    g_lo = (r1 - g_mid.astype(F32)).astype(BF16)
    g3 = jnp.concatenate([g_hi, g_mid, g_lo], axis=1)
    e3 = jnp.dot(d_ref[0], g3, preferred_element_type=F32)
    e_all = e3[:, :HG_WIDTH] + (e3[:, HG_WIDTH:2 * HG_WIDTH] + e3[:, 2 * HG_WIDTH:])

    row = lax.broadcasted_iota(jnp.int32, (chunk, chunk), 0)
    col = lax.broadcasted_iota(jnp.int32, (chunk, chunk), 1)
    x = row ^ col
    earlier = (row - col) * (1 - 2 * direction) > 0
    n_blk = len(levels) + 1

    for hd in range(HG_HEADS):
        ln = slice(hd * HG_DK, (hd + 1) * HG_DK)
        q, k, v, e = q_all[:, ln], k_all[:, ln], v_all[:, ln], e_all[:, ln]
        b = e[:chunk]
        tot = e[n_blk * chunk:n_blk * chunk + 1]
        st = st_ref[hd]
        vb = v.astype(BF16)
        o = lax.dot_general((q * jnp.exp(b)).astype(BF16), st.astype(BF16), NT_DIMS,
                            preferred_element_type=F32)
        a = jnp.where(x == 0, jnp.sum(q * k, axis=-1, keepdims=True), 0.0)
        for li, m in enumerate(levels):
            w = jnp.exp(e[(li + 1) * chunk:(li + 2) * chunk])
            al = lax.dot_general((q * w).astype(BF16), (k * w).astype(BF16), NT_DIMS,
                                 preferred_element_type=F32)
            a = a + jnp.where(earlier & (x >= m) & (x < 2 * m), al, 0.0)
        o_ref[0, 0, :, ln] = o + jnp.dot(a.astype(BF16), vb, preferred_element_type=F32)
        kd = (k * jnp.exp(jnp.minimum(tot - b, 0.0))).astype(BF16)
        st_ref[hd] = st * jnp.exp(tot) + lax.dot_general(vb, kd, TN_DIMS,
                                                         preferred_element_type=F32)


def _hgrn_call(uh, lb3, dmat, chunk):
    B, S, _ = uh.shape
    n = S // chunk

    def t_idx(d, i):
        return i + d * (n - 1 - 2 * i)

    return pl.pallas_call(
        functools.partial(_hgrn_kernel, chunk=chunk),
        grid=(2, B, n),
        in_specs=[
            pl.BlockSpec((1, chunk, HG_WIDTH), lambda d, b, i: (b, t_idx(d, i), 0)),
            pl.BlockSpec((1, chunk, HG_WIDTH), lambda d, b, i: (b, t_idx(d, i), 1)),
            pl.BlockSpec((1, chunk, HG_WIDTH), lambda d, b, i: (b, t_idx(d, i), 2 + d)),
            pl.BlockSpec((1, 1, HG_WIDTH), lambda d, b, i: (d, 0, 0)),
            pl.BlockSpec((1,) + dmat.shape[1:], lambda d, b, i: (d, 0, 0)),
        ],
        out_specs=pl.BlockSpec((1, 1, chunk, HG_WIDTH), lambda d, b, i: (d, b, t_idx(d, i), 0)),
        out_shape=jax.ShapeDtypeStruct((2, B, S, HG_WIDTH), F32),
        scratch_shapes=[pltpu.VMEM((HG_HEADS, HG_DV, HG_DK), F32)],
        compiler_params=pltpu.CompilerParams(
            dimension_semantics=("arbitrary", "arbitrary", "arbitrary"),
            vmem_limit_bytes=VMEM_LIMIT),
        name="hgrn",
    )(uh, uh, uh, lb3, dmat)


def _post_kernel(h1_ref, ot_ref, of_ref, ob_ref, hg_ref, p_ref, hgn_ref, wot_ref, wob_ref,
                 n2_ref, wg_ref, wu_ref, wd_ref, npl_ref, wpg_ref, wpp_ref, nf_ref, y_ref):
    o = of_ref[0, 0] + ob_ref[0, 0]
    gt = hg_ref[0]
    parts = []
    for hd in range(HG_HEADS):
        oh = o[:, hd * HG_DV:(hd + 1) * HG_DV]
        parts.append(_rms(oh, hgn_ref[:, hd * HG_DV:(hd + 1) * HG_DV]))
    hmix = (jnp.concatenate(parts, axis=1) * (gt * _sigmoid(gt))).astype(BF16)
    h = (h1_ref[0]
         + lax.dot_general(ot_ref[0].astype(BF16), wot_ref[...], TN_DIMS, preferred_element_type=F32)
         + jnp.dot(hmix, wob_ref[...], preferred_element_type=F32))
    h = h + 0.5 * _swiglu(_rms(h, n2_ref[...]).astype(BF16), wg_ref, wu_ref, wd_ref)
    gate = _sigmoid(jnp.dot(_rms(h, npl_ref[...]).astype(BF16), wpg_ref[...],
                            preferred_element_type=F32))
    h = h + gate * jnp.dot(p_ref[0, 0].astype(BF16), wpp_ref[...], preferred_element_type=F32)
    y_ref[0] = _rms(h, nf_ref[...])


def _post_call(h1, ot, o2, uh, p, layer, wts, tm):
    B, S, D = h1.shape
    g_blk = (uh.shape[2] - HG_WIDTH) // HG_WIDTH
    vm = pl.BlockSpec(memory_space=pltpu.VMEM)
    in_specs = [
        pl.BlockSpec((1, tm, D), lambda b, i: (b, i, 0)),
        pl.BlockSpec((1, MLA_WIDTH, tm), lambda b, i: (b, 0, i)),
        pl.BlockSpec((1, 1, tm, HG_WIDTH), lambda b, i: (0, b, i, 0)),
        pl.BlockSpec((1, 1, tm, HG_WIDTH), lambda b, i: (1, b, i, 0)),
        pl.BlockSpec((1, tm, HG_WIDTH), lambda b, i: (b, i, g_blk)),
        pl.BlockSpec((1, 1, tm, p.shape[-1]), lambda b, i: (layer, b, i, 0)),
    ] + [vm] * 11
    return pl.pallas_call(
        _post_kernel,
        grid=(B, S // tm),
        in_specs=in_specs,
        out_specs=pl.BlockSpec((1, tm, D), lambda b, i: (b, i, 0)),
        out_shape=jax.ShapeDtypeStruct((B, S, D), F32),
        compiler_params=pltpu.CompilerParams(
            dimension_semantics=("arbitrary", "arbitrary"), vmem_limit_bytes=VMEM_LIMIT),
        name="post",
    )(h1, ot, o2, o2, uh, p, wts["hg_norm"], wts["w_o_top"], wts["w_o_bot"], wts["ffn2_norm"],
      wts["ffn2_wg"], wts["ffn2_wu"], wts["ffn2_wd"], wts["ple_norm"], wts["w_ple_gate"],
      wts["w_ple_proj"], wts["final_norm"])


def _rope_tables(S):
    half = QK_ROPE // 2
    inv_freq = jnp.exp(jnp.arange(0, QK_ROPE, 2, dtype=F32) * (-math.log(ROPE_BASE) / QK_ROPE))
    ang = jnp.arange(S, dtype=F32)[:, None] * inv_freq[None, :]
    cos, sin = jnp.cos(ang), jnp.sin(ang)
    ones = jnp.ones((S, QK_NOPE), F32)
    zpad = jnp.zeros((S, HEAD_PAD - QK_NOPE - QK_ROPE), F32)
    c = jnp.concatenate([ones, cos, cos, zpad], axis=1)
    s = jnp.concatenate([0.0 * ones, -sin, sin, zpad], axis=1)
    assert half * 2 == QK_ROPE
    return c.T, s.T, c, s


def _layer_weights(l, ffn1_norm, ffn1_wg, ffn1_wu, ffn1_wd, mix_norm, w_in, q_norm, w_uq, kv_norm,
                   w_uk, w_uv, hg_norm, w_o, ffn2_norm, ffn2_wg, ffn2_wu, ffn2_wd, ple_norm,
                   w_ple_gate, w_ple_proj, final_norm):
    D = w_in.shape[1]
    half = QK_ROPE // 2
    o_kr = Q_LORA + KV_LORA
    wi = w_in[l]
    z64 = jnp.zeros((D, QK_NOPE), F32)
    z32 = jnp.zeros((D, HEAD_PAD - QK_NOPE - QK_ROPE), F32)
    x1, x2 = wi[:, o_kr:o_kr + half], wi[:, o_kr + half:o_kr + QK_ROPE]
    w_in_p = jnp.concatenate([wi[:, :o_kr], z64, x1, x2, z32, z64, x2, x1, z32,
                              wi[:, o_kr + QK_ROPE:]], axis=1)

    wq = w_uq[l].reshape(Q_LORA, MLA_HEADS, QK_NOPE + QK_ROPE)
    qn, q1, q2 = wq[..., :QK_NOPE], wq[..., QK_NOPE:QK_NOPE + half], wq[..., QK_NOPE + half:]
    zq = jnp.zeros((Q_LORA, MLA_HEADS, HEAD_PAD - QK_NOPE - QK_ROPE), F32)
    wqa = jnp.concatenate([qn, q1, q2, zq], axis=-1).reshape(Q_LORA, MLA_HEADS * HEAD_PAD)
    wqb = jnp.concatenate([0.0 * qn, q2, q1, zq], axis=-1).reshape(Q_LORA, MLA_HEADS * HEAD_PAD)

    wk = w_uk[l].reshape(KV_LORA, MLA_HEADS, QK_NOPE)
    wkp = jnp.concatenate([wk, jnp.zeros((KV_LORA, MLA_HEADS, HEAD_PAD - QK_NOPE), F32)],
                          axis=-1).reshape(KV_LORA, MLA_HEADS * HEAD_PAD)

    row = lambda a: a.reshape(1, -1)
    return {
        "ffn1_norm": row(ffn1_norm[l]), "ffn1_wg": ffn1_wg[l].astype(BF16),
        "ffn1_wu": ffn1_wu[l].astype(BF16), "ffn1_wd": ffn1_wd[l].astype(BF16),
        "mix_norm": row(mix_norm[l]), "w_in": w_in_p.astype(BF16),
        "q_norm": row(q_norm[l]), "w_qat": wqa.T.astype(BF16), "w_qbt": wqb.T.astype(BF16),
        "kv_norm": row(kv_norm[l]), "w_ukp": wkp.astype(BF16), "w_uvt": w_uv[l].T.astype(BF16),
        "hg_norm": row(hg_norm[l]),
        "w_o_top": w_o[l][:MLA_WIDTH].astype(BF16), "w_o_bot": w_o[l][MLA_WIDTH:].astype(BF16),
        "ffn2_norm": row(ffn2_norm[l]), "ffn2_wg": ffn2_wg[l].astype(BF16),
        "ffn2_wu": ffn2_wu[l].astype(BF16), "ffn2_wd": ffn2_wd[l].astype(BF16),
        "ple_norm": row(ple_norm[l]), "w_ple_gate": w_ple_gate[l].astype(BF16),
        "w_ple_proj": w_ple_proj[l].astype(BF16), "final_norm": row(final_norm),
    }


def _tiles(S):
    tm = min(512, S)
    tq = min(512, S)
    tk = min(512, S)
    chunk = min(128, S)
    return tm, tq, tk, chunk


def _trunk(x, p, layers, lb3, depth):
    B, S, D = x.shape
    tm, tq, tk, chunk = _tiles(S)
    tabs = _rope_tables(S)
    dmat = jnp.asarray(_hgrn_consts(chunk), BF16)
    h = x
    for l in range(depth):
        wts = layers[l]
        h1, qt, k, vt, uh = _pre_call(h, tabs, wts, tm)
        ot = _attn_call(qt, k, vt, tq, tk)
        o2 = _hgrn_call(uh, lb3[l], dmat, chunk)
        h = _post_call(h1, ot, o2, uh, p, l, wts, tm)
    return h


def kernel(x_prompt, x_sample, p_prompt, p_sample, ffn1_norm, ffn1_wg, ffn1_wu, ffn1_wd, mix_norm,
           w_in, q_norm, w_uq, kv_norm, w_uk, w_uv, hg_lb, hg_norm, w_o, ffn2_norm, ffn2_wg,
           ffn2_wu, ffn2_wd, ple_norm, w_ple_gate, w_ple_proj, final_norm):
    depth = w_in.shape[0]
    assert depth == 1, "the final norm is fused into the last layer's post kernel"
    layers = [
        _layer_weights(l, ffn1_norm, ffn1_wg, ffn1_wu, ffn1_wd, mix_norm, w_in, q_norm, w_uq,
                       kv_norm, w_uk, w_uv, hg_norm, w_o, ffn2_norm, ffn2_wg, ffn2_wu, ffn2_wd,
                       ple_norm, w_ple_gate, w_ple_proj, final_norm)
        for l in range(depth)
    ]
    lb = jnp.cumsum(jax.nn.softmax(hg_lb.astype(F32), axis=1), axis=1)
    lb3 = [jnp.stack([lb[0, l], lb[1, l]]).reshape(2, 1, HG_WIDTH) for l in range(depth)]
    y_prompt = _trunk(x_prompt, p_prompt, layers, lb3, depth)
    y_sample = _trunk(x_sample, p_sample, layers, lb3, depth)
    return (y_prompt, y_sample)
```
